```python
import math
import jax, jax.numpy as jnp
from jax import lax
import numpy as np

D_MODEL = 2048
BATCH = 2
SEQ = 16384
DEPTH = 1

N_Q_HEADS = 16
N_KV_HEADS = 4
HEAD_DIM = 64
GROUP = N_Q_HEADS // N_KV_HEADS
Q_DIM = N_Q_HEADS * HEAD_DIM
KV_DIM = N_KV_HEADS * HEAD_DIM
WINDOW = 128
BLOCK = 128
CONV_CH = D_MODEL - Q_DIM
CONV_K = 31
MIX_DIM = Q_DIM + CONV_CH
IN_COLS = Q_DIM + 2 * KV_DIM + 2 * CONV_CH
D_FF = 5632
FFN_CONV_K = 3
EPS = 1e-6

kernel_name = "hymba_swa_sink_conformer_convffn"


def rms_norm(x, w):
    xf = x.astype(jnp.float32)
    y = xf * lax.rsqrt(jnp.mean(xf * xf, axis=-1, keepdims=True) + EPS)
    return (y * w.astype(jnp.float32)).astype(x.dtype)


def layer_norm(x, w, b):
    xf = x.astype(jnp.float32)
    mu = jnp.mean(xf, axis=-1, keepdims=True)
    var = jnp.mean(jnp.square(xf - mu), axis=-1, keepdims=True)
    y = (xf - mu) * lax.rsqrt(var + EPS)
    return (y * w.astype(jnp.float32) + b.astype(jnp.float32)).astype(x.dtype)


def causal_depthwise_conv(x, w, b):
    k, c = w.shape
    y = lax.conv_general_dilated(
        x, w[:, None, :].astype(x.dtype), window_strides=(1,), padding=[(k - 1, 0)],
        dimension_numbers=("NWC", "WIO", "NWC"), feature_group_count=c)
    return y + b.astype(x.dtype)


def sliding_window_sink_attention(q, k, v, sinks):
    b, s = q.shape[0], q.shape[1]
    nb = s // BLOCK
    qb = q.reshape(b, nb, BLOCK, N_KV_HEADS, GROUP, HEAD_DIM)

    def band(t):
        tp = jnp.pad(t, ((0, 0), (BLOCK, 0), (0, 0), (0, 0)))[:, :s]
        prev = tp.reshape(b, nb, BLOCK, N_KV_HEADS, HEAD_DIM)
        cur = t.reshape(b, nb, BLOCK, N_KV_HEADS, HEAD_DIM)
        return jnp.concatenate([prev, cur], axis=2)

    kb, vb = band(k), band(v)
    scale = 1.0 / math.sqrt(HEAD_DIM)
    scores = jnp.einsum("bnqhgd,bnkhd->bnhgqk", qb, kb,
                        preferred_element_type=jnp.float32) * scale
    qi = jnp.arange(BLOCK)[:, None] + BLOCK
    kj = jnp.arange(2 * BLOCK)[None, :]
    diff = qi - kj
    in_window = (diff >= 0) & (diff < WINDOW)
    key_pos = jnp.arange(nb)[:, None] * BLOCK - BLOCK + jnp.arange(2 * BLOCK)[None, :]
    valid = in_window[None] & (key_pos >= 0)[:, None, :]
    scores = jnp.where(valid[None, :, None, None], scores, -jnp.inf)
    sink = jnp.broadcast_to(sinks.astype(jnp.float32).reshape(1, 1, N_KV_HEADS, GROUP, 1, 1),
                            scores.shape[:-1] + (1,))
    probs = jax.nn.softmax(jnp.concatenate([scores, sink], axis=-1), axis=-1)[..., :-1]
    out = jnp.einsum("bnhgqk,bnkhd->bnqhgd", probs.astype(v.dtype), vb)
    return out.reshape(b, s, Q_DIM)


def hybrid_layer(x, attn_norm_w, w_in, q_norm_w, k_norm_w, sinks, conv_dw_w, conv_dw_b,
                 conv_ln_w, conv_ln_b, w_out, ffn_norm_w, w_up, ffn_dw_w, ffn_dw_b, w_down):
    b, s, _ = x.shape
    h = rms_norm(x, attn_norm_w)
    p = h @ w_in
    q, k, v, conv_in = jnp.split(p, [Q_DIM, Q_DIM + KV_DIM, Q_DIM + 2 * KV_DIM], axis=-1)
    q = rms_norm(q.reshape(b, s, N_KV_HEADS, GROUP, HEAD_DIM), q_norm_w)
    k = rms_norm(k.reshape(b, s, N_KV_HEADS, HEAD_DIM), k_norm_w)
    v = v.reshape(b, s, N_KV_HEADS, HEAD_DIM)
    attn_out = sliding_window_sink_attention(q, k, v, sinks)
    a, g = jnp.split(conv_in, 2, axis=-1)
    c = a * jax.nn.sigmoid(g)
    c = causal_depthwise_conv(c, conv_dw_w, conv_dw_b)
    c = jax.nn.silu(layer_norm(c, conv_ln_w, conv_ln_b))
    x = x + jnp.concatenate([attn_out, c], axis=-1) @ w_out
    h = rms_norm(x, ffn_norm_w)
    u = causal_depthwise_conv(h @ w_up, ffn_dw_w, ffn_dw_b)
    gate, val = jnp.split(u, 2, axis=-1)
    return x + (jax.nn.silu(gate) * val) @ w_down


def setup_inputs(seed: int = 0) -> dict:
    key = jax.random.key(seed)
    ks = jax.random.split(key, 17)
    f32 = jnp.float32
    L = DEPTH

    def nrm(k, shape, scale):
        return jax.random.normal(k, shape, f32) * scale

    def gain(k, shape):
        return 1.0 + 0.02 * jax.random.normal(k, shape, f32)

    return {
        "x": jax.random.normal(ks[0], (BATCH, SEQ, D_MODEL), f32),
        "attn_norm_w": gain(ks[1], (L, D_MODEL)),
        "w_in": nrm(ks[2], (L, D_MODEL, IN_COLS), D_MODEL ** -0.5),
        "q_norm_w": gain(ks[3], (L, HEAD_DIM)),
        "k_norm_w": gain(ks[4], (L, HEAD_DIM)),
        "sinks": nrm(ks[5], (L, N_Q_HEADS), 0.5),
        "conv_dw_w": nrm(ks[6], (L, CONV_K, CONV_CH), CONV_K ** -0.5),
        "conv_dw_b": nrm(ks[7], (L, CONV_CH), 0.02),
        "conv_ln_w": gain(ks[8], (L, CONV_CH)),
        "conv_ln_b": nrm(ks[9], (L, CONV_CH), 0.02),
        "w_out": nrm(ks[10], (L, MIX_DIM, D_MODEL), MIX_DIM ** -0.5),
        "ffn_norm_w": gain(ks[11], (L, D_MODEL)),
        "w_up": nrm(ks[12], (L, D_MODEL, 2 * D_FF), D_MODEL ** -0.5),
        "ffn_dw_w": nrm(ks[13], (L, FFN_CONV_K, 2 * D_FF), FFN_CONV_K ** -0.5),
        "ffn_dw_b": nrm(ks[14], (L, 2 * D_FF), 0.02),
        "w_down": nrm(ks[15], (L, D_FF, D_MODEL), D_FF ** -0.5),
    }


def reference(x, attn_norm_w, w_in, q_norm_w, k_norm_w, sinks, conv_dw_w, conv_dw_b,
              conv_ln_w, conv_ln_b, w_out, ffn_norm_w, w_up, ffn_dw_w, ffn_dw_b, w_down):
    for l in range(DEPTH):
        x = hybrid_layer(x, attn_norm_w[l], w_in[l], q_norm_w[l], k_norm_w[l], sinks[l],
                         conv_dw_w[l], conv_dw_b[l], conv_ln_w[l], conv_ln_b[l], w_out[l],
                         ffn_norm_w[l], w_up[l], ffn_dw_w[l], ffn_dw_b[l], w_down[l])
    return x
```

```python
import functools

import jax
import jax.numpy as jnp
from jax import lax
from jax.experimental import pallas as pl
from jax.experimental.pallas import tpu as pltpu

F32 = jnp.float32
BF16 = jnp.bfloat16

D_MODEL = 2048
N_Q_HEADS = 16
N_KV_HEADS = 4
HEAD_DIM = 64
GROUP = N_Q_HEADS // N_KV_HEADS
Q_DIM = N_Q_HEADS * HEAD_DIM
KV_DIM = N_KV_HEADS * HEAD_DIM
WINDOW = 128
CONV_CH = D_MODEL - Q_DIM
CONV_K = 31
D_FF = 5632
FFN_CONV_K = 3
EPS = 1e-6

LANES = 128
SUBLANES = 8
MXU_DIM = 256
VMEM_LIMIT = 56 * 1024 * 1024

TM_PROJ = 512
TQ_ATTN = 512
TS_CONV = 512
CONV_HALO = 32
CONV_ROWS = 64
TM_FFN = 512
FC_FFN = 512
N_FC = D_FF // FC_FFN


def _dot(a, b):
    return jnp.dot(a, b, preferred_element_type=F32)


def _dot_nt(a, b):
    return lax.dot_general(a, b, (((1,), (1,)), ((), ())), preferred_element_type=F32)


def _resident(shape):
    return pl.BlockSpec(shape, lambda *_: (0,) * len(shape), pipeline_mode=pl.Buffered(1))


def _inproj_kernel(x_ref, nw_ref, w_ref, g_ref, qw_ref, kw_ref,
                   q_ref, k_ref, ks_ref, v_ref, vs_ref, c_ref):
    x = x_ref[...]
    ms = jnp.mean(x * x, axis=-1, keepdims=True)
    h = (x * lax.rsqrt(ms + EPS) * nw_ref[...]).astype(BF16)
    g = g_ref[...]

    def head_norm(t, w):
        sq = t * t
        hi = sq.astype(BF16)
        lo = (sq - hi.astype(F32)).astype(BF16)
        hms = _dot(hi, g) + _dot(lo, g)
        return t * lax.rsqrt(hms + EPS) * w

    def swap_halves(t):
        return jnp.concatenate(
            [pltpu.roll(t[:, i * LANES:(i + 1) * LANES], HEAD_DIM, axis=1)
             for i in range(t.shape[1] // LANES)], axis=1)

    scale = 1.0 / (HEAD_DIM ** 0.5)
    for i in range(Q_DIM // MXU_DIM):
        sl = slice(i * MXU_DIM, (i + 1) * MXU_DIM)
        q = _dot(h, w_ref[:, sl])
        q_ref[:, sl] = (head_norm(q, qw_ref[...]) * scale).astype(BF16)
    k = head_norm(_dot(h, w_ref[:, Q_DIM:Q_DIM + KV_DIM]), kw_ref[...])
    k_ref[...] = k.astype(BF16)
    ks_ref[...] = swap_halves(k).astype(BF16)
    v = _dot(h, w_ref[:, Q_DIM + KV_DIM:Q_DIM + 2 * KV_DIM])
    v_ref[...] = v.astype(BF16)
    vs_ref[...] = swap_halves(v).astype(BF16)
    a0 = Q_DIM + 2 * KV_DIM
    g0 = a0 + CONV_CH
    for i in range(CONV_CH // MXU_DIM):
        a = _dot(h, w_ref[:, a0 + i * MXU_DIM:a0 + (i + 1) * MXU_DIM])
        gt = _dot(h, w_ref[:, g0 + i * MXU_DIM:g0 + (i + 1) * MXU_DIM])
        c_ref[:, i * MXU_DIM:(i + 1) * MXU_DIM] = a * jax.nn.sigmoid(gt)


def _inproj(x2, nw, w_in, gmat, qw, kw):
    m = x2.shape[0]
    in_cols = w_in.shape[1]
    row = lambda n: pl.BlockSpec((TM_PROJ, n), lambda i: (i, 0))
    return pl.pallas_call(
        _inproj_kernel,
        grid=(m // TM_PROJ,),
        in_specs=[row(D_MODEL), _resident((1, D_MODEL)), _resident((D_MODEL, in_cols)),
                  _resident((MXU_DIM, MXU_DIM)), _resident((1, MXU_DIM)), _resident((1, KV_DIM))],
        out_specs=[row(Q_DIM), row(KV_DIM), row(KV_DIM), row(KV_DIM), row(KV_DIM), row(CONV_CH)],
        out_shape=[jax.ShapeDtypeStruct((m, Q_DIM), BF16)] + [jax.ShapeDtypeStruct((m, KV_DIM), BF16)] * 4
        + [jax.ShapeDtypeStruct((m, CONV_CH), F32)],
        compiler_params=pltpu.CompilerParams(dimension_semantics=("arbitrary",), vmem_limit_bytes=VMEM_LIMIT),
        name="inproj",
    )(x2, nw, w_in, gmat, qw, kw)


def _attn_kernel(sink_ref, q_ref, k_ref, ks_ref, v_ref, vs_ref, kp_ref, ksp_ref, vp_ref, vsp_ref,
                 o_ref, kbuf, ksbuf, vbuf, vsbuf):
    i = pl.program_id(1)
    for buf, prev, cur in ((kbuf, kp_ref, k_ref), (ksbuf, ksp_ref, ks_ref),
                           (vbuf, vp_ref, v_ref), (vsbuf, vsp_ref, vs_ref)):
        buf[0:WINDOW, :] = prev[0]
        buf[WINDOW:, :] = cur[0]

    r = lax.broadcasted_iota(jnp.int32, (WINDOW, WINDOW), 0)
    c = lax.broadcasted_iota(jnp.int32, (WINDOW, WINDOW), 1)
    upper = c > r
    lo = lax.broadcasted_iota(jnp.int32, (2 * WINDOW, LANES), 1) < HEAD_DIM

    def block(b, carry):
        row0 = pl.multiple_of(b * WINDOW, WINDOW)
        prev_bias = jnp.where(jnp.logical_and(i == 0, b == 0), -jnp.inf, 0.0).astype(F32)
        for hp in range(N_KV_HEADS // 2):
            ls = slice(hp * LANES, (hp + 1) * LANES)
            kcat = kbuf[pl.ds(row0, 2 * WINDOW), ls]
            kscat = ksbuf[pl.ds(row0, 2 * WINDOW), ls]
            vcat = vbuf[pl.ds(row0, 2 * WINDOW), ls]
            vscat = vsbuf[pl.ds(row0, 2 * WINDOW), ls]
            zero = jnp.zeros_like(kcat)
            for half in range(2):
                h = 2 * hp + half
                own_k, oth_k = (kcat, kscat) if half == 0 else (kscat, kcat)
                own_v, oth_v = (vcat, vscat) if half == 0 else (vscat, vcat)
                k_a = jnp.where(lo, own_k, zero)
                k_b = jnp.where(lo, zero, oth_k)
                v_a = jnp.where(lo, own_v, zero)
                v_b = jnp.where(lo, zero, oth_v)
                for pair in range(GROUP // 2):
                    col = h * GROUP * HEAD_DIM + pair * LANES
                    qp = q_ref[0, pl.ds(row0, WINDOW), col:col + LANES]
                    out = jnp.zeros((WINDOW, LANES), F32)
                    for k_x, v_x, gi in ((k_a, v_a, 2 * pair), (k_b, v_b, 2 * pair + 1)):
                        s = _dot_nt(qp, k_x)
                        s_prev = s[:, :WINDOW] + prev_bias
                        s_cur = s[:, WINDOW:]
                        sc = jnp.where(upper, s_prev, s_cur)
                        sink = sink_ref[h * GROUP + gi]
                        m = jnp.maximum(jnp.max(sc, axis=-1, keepdims=True), sink)
                        e = jnp.exp(sc - m)
                        den = jnp.sum(e, axis=-1, keepdims=True) + jnp.exp(sink - m)
                        p = e * (1.0 / den)
                        pz = jnp.zeros_like(p)
                        pcat = jnp.concatenate([jnp.where(upper, p, pz), jnp.where(upper, pz, p)], axis=1)
                        out = out + _dot(pcat.astype(BF16), v_x)
                    o_ref[0, pl.ds(row0, WINDOW), col:col + LANES] = out.astype(BF16)
        return carry

    lax.fori_loop(0, TQ_ATTN // WINDOW, block, 0)


def _attention(sinks, q, k, ks, v, vs):
    b, s, _ = q.shape
    nprev = TQ_ATTN // WINDOW
    cur = lambda n: pl.BlockSpec((1, TQ_ATTN, n), lambda bi, i: (bi, i, 0))
    prev = pl.BlockSpec((1, WINDOW, KV_DIM), lambda bi, i: (bi, jnp.maximum(i * nprev - 1, 0), 0))
    kv_scratch = pltpu.VMEM((TQ_ATTN + WINDOW, KV_DIM), BF16)
    return pl.pallas_call(
        _attn_kernel,
        grid=(b, s // TQ_ATTN),
        in_specs=[pl.BlockSpec(memory_space=pltpu.SMEM), cur(Q_DIM), cur(KV_DIM), cur(KV_DIM), cur(KV_DIM),
                  cur(KV_DIM), prev, prev, prev, prev],
        out_specs=cur(Q_DIM),
        out_shape=jax.ShapeDtypeStruct((b, s, Q_DIM), BF16),
        scratch_shapes=[kv_scratch] * 4,
        compiler_params=pltpu.CompilerParams(dimension_semantics=("arbitrary", "arbitrary"),
                                             vmem_limit_bytes=VMEM_LIMIT),
        name="swa_attention",
    )(sinks, q, k, ks, v, vs, k, ks, v, vs)


def _conv_kernel(c_ref, cp_ref, w_ref, b_ref, lw_ref, lb_ref, o_ref, cbuf, sbuf, ybuf):
    i = pl.program_id(1)
    halo = cp_ref[0]
    cbuf[0:CONV_HALO, :] = jnp.where(i == 0, jnp.zeros_like(halo), halo)
    cbuf[CONV_HALO:, :] = c_ref[0]
    off = CONV_HALO - (CONV_K - 1)
    srows = sbuf.shape[1]

    for cb in range(CONV_CH // LANES):
        ls = slice(cb * LANES, (cb + 1) * LANES)
        for r in range(1, SUBLANES):
            sbuf[r - 1] = cbuf[r:r + srows, ls]

        def taps(rb, carry, ls=ls):
            r0 = pl.multiple_of(rb * CONV_ROWS, CONV_ROWS)
            acc = jnp.broadcast_to(b_ref[:, ls], (CONV_ROWS, LANES))
            for t in range(CONV_K):
                a, r = divmod(off + t, SUBLANES)
                if r == 0:
                    src = cbuf[pl.ds(r0 + a * SUBLANES, CONV_ROWS), ls]
                else:
                    src = sbuf[r - 1, pl.ds(r0 + a * SUBLANES, CONV_ROWS), :]
                acc = acc + w_ref[t:t + 1, ls] * src
            ybuf[pl.ds(r0, CONV_ROWS), ls] = acc
            return carry

        lax.fori_loop(0, TS_CONV // CONV_ROWS, taps, 0)

    def rows(rb, carry):
        r0 = pl.multiple_of(rb * CONV_ROWS, CONV_ROWS)
        y = ybuf[pl.ds(r0, CONV_ROWS), :]
        mu = jnp.mean(y, axis=-1, keepdims=True)
        d = y - mu
        var = jnp.mean(d * d, axis=-1, keepdims=True)
        z = d * lax.rsqrt(var + EPS) * lw_ref[...] + lb_ref[...]
        o_ref[0, pl.ds(r0, CONV_ROWS), :] = (z * jax.nn.sigmoid(z)).astype(BF16)
        return carry

    lax.fori_loop(0, TS_CONV // CONV_ROWS, rows, 0)


def _conv_module(c, w, bias, lw, lb):
    b, s, ch = c.shape
    nh = TS_CONV // CONV_HALO
    return pl.pallas_call(
        _conv_kernel,
        grid=(b, s // TS_CONV),
        in_specs=[pl.BlockSpec((1, TS_CONV, ch), lambda bi, i: (bi, i, 0)),
                  pl.BlockSpec((1, CONV_HALO, ch), lambda bi, i: (bi, jnp.maximum(i * nh - 1, 0), 0)),
                  _resident((CONV_K, ch)), _resident((1, ch)), _resident((1, ch)), _resident((1, ch))],
        out_specs=pl.BlockSpec((1, TS_CONV, ch), lambda bi, i: (bi, i, 0)),
        out_shape=jax.ShapeDtypeStruct((b, s, ch), BF16),
        scratch_shapes=[pltpu.VMEM((TS_CONV + CONV_HALO, ch), F32),
                        pltpu.VMEM((SUBLANES - 1, TS_CONV + CONV_HALO - SUBLANES, LANES), F32),
                        pltpu.VMEM((TS_CONV, ch), F32)],
        compiler_params=pltpu.CompilerParams(dimension_semantics=("arbitrary", "arbitrary"),
                                             vmem_limit_bytes=VMEM_LIMIT),
        name="conformer_conv",
    )(c, c, w, bias, lw, lb)


def _outproj_kernel(a_ref, c_ref, x_ref, w_ref, nw_ref, x1_ref, h_ref):
    x1 = x_ref[...] + _dot(a_ref[...], w_ref[0:Q_DIM, :]) + _dot(c_ref[...], w_ref[Q_DIM:, :])
    x1_ref[...] = x1
    ms = jnp.mean(x1 * x1, axis=-1, keepdims=True)
    h_ref[...] = (x1 * lax.rsqrt(ms + EPS) * nw_ref[...]).astype(BF16)


def _outproj(attn, conv, x2, w_out, nw):
    m = x2.shape[0]
    row = lambda n: pl.BlockSpec((TM_PROJ, n), lambda i: (i, 0))
    return pl.pallas_call(
        _outproj_kernel,
        grid=(m // TM_PROJ,),
        in_specs=[row(Q_DIM), row(CONV_CH), row(D_MODEL), _resident((D_MODEL, D_MODEL)), _resident((1, D_MODEL))],
        out_specs=[row(D_MODEL), row(D_MODEL)],
        out_shape=[jax.ShapeDtypeStruct((m, D_MODEL), F32), jax.ShapeDtypeStruct((m, D_MODEL), BF16)],
        compiler_params=pltpu.CompilerParams(dimension_semantics=("arbitrary",), vmem_limit_bytes=VMEM_LIMIT),
        name="outproj",
    )(attn, conv, x2, w_out, nw)


def _ffn_kernel(h_ref, x1_ref, wu_ref, cw_ref, cb_ref, wd_ref, o_ref, ubuf, carry, *, tiles_per_seq):
    i = pl.program_id(0)
    j = pl.program_id(1)
    hist = SUBLANES
    u = _dot(h_ref[...], wu_ref[0])
    prev = carry[j]
    ubuf[0:hist, :] = jnp.where(i % tiles_per_seq == 0, jnp.zeros_like(prev), prev)
    ubuf[hist:, :] = u
    carry[j] = u[TM_FFN - hist:, :]
    y = cb_ref[0] + cw_ref[0, 2:3, :] * u
    for t in range(FFN_CONV_K - 1):
        shift = FFN_CONV_K - 1 - t
        y = y + cw_ref[0, t:t + 1, :] * ubuf[hist - shift:hist - shift + TM_FFN, :]
    gate = y[:, :FC_FFN]
    val = y[:, FC_FFN:]
    act = (gate * jax.nn.sigmoid(gate) * val).astype(BF16)
    d = _dot(act, wd_ref[...])

    @pl.when(j == 0)
    def _():
        o_ref[...] = x1_ref[...] + d

    @pl.when(j > 0)
    def _():
        o_ref[...] += d


def _ffn(h2, x1, wu, cw, cb, wd, tiles_per_seq):
    m = h2.shape[0]
    row = lambda n: pl.BlockSpec((TM_FFN, n), lambda i, j: (i, 0))
    return pl.pallas_call(
        functools.partial(_ffn_kernel, tiles_per_seq=tiles_per_seq),
        grid=(m // TM_FFN, N_FC),
        in_specs=[row(D_MODEL), row(D_MODEL),
                  pl.BlockSpec((1, D_MODEL, 2 * FC_FFN), lambda i, j: (j, 0, 0)),
                  pl.BlockSpec((1, FFN_CONV_K, 2 * FC_FFN), lambda i, j: (j, 0, 0)),
                  pl.BlockSpec((1, 1, 2 * FC_FFN), lambda i, j: (j, 0, 0)),
                  pl.BlockSpec((FC_FFN, D_MODEL), lambda i, j: (j, 0))],
        out_specs=row(D_MODEL),
        out_shape=jax.ShapeDtypeStruct((m, D_MODEL), F32),
        scratch_shapes=[pltpu.VMEM((TM_FFN + SUBLANES, 2 * FC_FFN), F32),
                        pltpu.VMEM((N_FC, SUBLANES, 2 * FC_FFN), F32)],
        compiler_params=pltpu.CompilerParams(dimension_semantics=("arbitrary", "arbitrary"),
                                             vmem_limit_bytes=VMEM_LIMIT),
        name="conv_ffn",
    )(h2, x1, wu, cw, cb, wd)


def _chunk_gate_value(t):
    lead = t.shape[:-1]
    t = t.reshape(lead + (2, N_FC, FC_FFN))
    t = jnp.moveaxis(t, -2, 0)
    return t.reshape((N_FC,) + lead + (2 * FC_FFN,))


def _layer(x, attn_norm_w, w_in, q_norm_w, k_norm_w, sinks, conv_dw_w, conv_dw_b, conv_ln_w, conv_ln_b,
           w_out, ffn_norm_w, w_up, ffn_dw_w, ffn_dw_b, w_down):
    b, s, d = x.shape
    m = b * s
    x2 = x.reshape(m, d)
    head = lax.broadcasted_iota(jnp.int32, (MXU_DIM, MXU_DIM), 0) // HEAD_DIM
    gmat = jnp.where(head == head.T, 1.0 / HEAD_DIM, 0.0).astype(BF16)
    qw = jnp.tile(q_norm_w, MXU_DIM // HEAD_DIM)[None, :]
    kw = jnp.tile(k_norm_w, KV_DIM // HEAD_DIM)[None, :]
    q, k, ks, v, vs, c = _inproj(x2, attn_norm_w[None, :], w_in.astype(BF16), gmat, qw, kw)
    seq = lambda t: t.reshape(b, s, t.shape[-1])
    attn = _attention(sinks, seq(q), seq(k), seq(ks), seq(v), seq(vs))
    conv = _conv_module(seq(c), conv_dw_w, conv_dw_b[None, :], conv_ln_w[None, :], conv_ln_b[None, :])
    x1, h2 = _outproj(attn.reshape(m, Q_DIM), conv.reshape(m, CONV_CH), x2, w_out.astype(BF16),
                      ffn_norm_w[None, :])
    out = _ffn(h2, x1, _chunk_gate_value(w_up.astype(BF16)), _chunk_gate_value(ffn_dw_w),
               _chunk_gate_value(ffn_dw_b[None, :]), w_down.astype(BF16), s // TM_FFN)
    return out.reshape(b, s, d)


def kernel(x, attn_norm_w, w_in, q_norm_w, k_norm_w, sinks, conv_dw_w, conv_dw_b, conv_ln_w, conv_ln_b, w_out,
           ffn_norm_w, w_up, ffn_dw_w, ffn_dw_b, w_down):
    for l in range(attn_norm_w.shape[0]):
        x = _layer(x, attn_norm_w[l], w_in[l], q_norm_w[l], k_norm_w[l], sinks[l], conv_dw_w[l], conv_dw_b[l],
                   conv_ln_w[l], conv_ln_b[l], w_out[l], ffn_norm_w[l], w_up[l], ffn_dw_w[l], ffn_dw_b[l],
                   w_down[l])
    return x
```

```python
import functools

import jax
import jax.numpy as jnp
from jax import lax
from jax.experimental import pallas as pl
from jax.experimental.pallas import tpu as pltpu

F32 = jnp.float32
BF16 = jnp.bfloat16

D_MODEL = 2048
N_Q_HEADS = 16
N_KV_HEADS = 4
HEAD_DIM = 64
GROUP = N_Q_HEADS // N_KV_HEADS
Q_DIM = N_Q_HEADS * HEAD_DIM
KV_DIM = N_KV_HEADS * HEAD_DIM
WINDOW = 128
CONV_CH = D_MODEL - Q_DIM
CONV_K = 31
D_FF = 5632
FFN_CONV_K = 3
EPS = 1e-6

LANES = 128
SUBLANES = 8
MXU_DIM = 256
VMEM_LIMIT = 56 * 1024 * 1024

TM_PROJ = 512
TQ_ATTN = 512
TS_CONV = 512
CONV_HALO = 32
CONV_ROWS = 64
TM_FFN = 1024
X1_SLABS = 8
FC_FFN = 512
N_FC = D_FF // FC_FFN


def _dot(a, b):
    return jnp.dot(a, b, preferred_element_type=F32)


def _dot_nt(a, b):
    return lax.dot_general(a, b, (((1,), (1,)), ((), ())), preferred_element_type=F32)


def _resident(shape):
    return pl.BlockSpec(shape, lambda *_: (0,) * len(shape), pipeline_mode=pl.Buffered(1))


def _inproj_kernel(x_ref, nw_ref, w_ref, g_ref, qw_ref, kw_ref,
                   q_ref, k_ref, ks_ref, v_ref, vs_ref, c_ref):
    x = x_ref[...]
    ms = jnp.mean(x * x, axis=-1, keepdims=True)
    h = (x * lax.rsqrt(ms + EPS) * nw_ref[...]).astype(BF16)
    g = g_ref[...]

    def head_norm(t, w):
        sq = t * t
        hi = sq.astype(BF16)
        lo = (sq - hi.astype(F32)).astype(BF16)
        hms = _dot(hi, g) + _dot(lo, g)
        return t * lax.rsqrt(hms + EPS) * w

    def swap_halves(t):
        return jnp.concatenate(
            [pltpu.roll(t[:, i * LANES:(i + 1) * LANES], HEAD_DIM, axis=1)
             for i in range(t.shape[1] // LANES)], axis=1)

    scale = 1.0 / (HEAD_DIM ** 0.5)
    qkv = _dot(h, w_ref[:, 0:Q_DIM + 2 * KV_DIM])
    for i in range(Q_DIM // MXU_DIM):
        sl = slice(i * MXU_DIM, (i + 1) * MXU_DIM)
        q_ref[:, sl] = (head_norm(qkv[:, sl], qw_ref[...]) * scale).astype(BF16)
    k = head_norm(qkv[:, Q_DIM:Q_DIM + KV_DIM], kw_ref[...])
    k_ref[...] = k.astype(BF16)
    ks_ref[...] = swap_halves(k).astype(BF16)
    v = qkv[:, Q_DIM + KV_DIM:Q_DIM + 2 * KV_DIM]
    v_ref[...] = v.astype(BF16)
    vs_ref[...] = swap_halves(v).astype(BF16)
    a0 = Q_DIM + 2 * KV_DIM
    ag = _dot(h, w_ref[:, a0:a0 + 2 * CONV_CH])
    c_ref[...] = ag[:, :CONV_CH] * jax.nn.sigmoid(ag[:, CONV_CH:])


def _inproj(x2, nw, w_in, gmat, qw, kw):
    m = x2.shape[0]
    in_cols = w_in.shape[1]
    row = lambda n: pl.BlockSpec((TM_PROJ, n), lambda i: (i, 0))
    return pl.pallas_call(
        _inproj_kernel,
        grid=(m // TM_PROJ,),
        in_specs=[row(D_MODEL), _resident((1, D_MODEL)), _resident((D_MODEL, in_cols)),
                  _resident((MXU_DIM, MXU_DIM)), _resident((1, MXU_DIM)), _resident((1, KV_DIM))],
        out_specs=[row(Q_DIM), row(KV_DIM), row(KV_DIM), row(KV_DIM), row(KV_DIM), row(CONV_CH)],
        out_shape=[jax.ShapeDtypeStruct((m, Q_DIM), BF16)] + [jax.ShapeDtypeStruct((m, KV_DIM), BF16)] * 4
        + [jax.ShapeDtypeStruct((m, CONV_CH), F32)],
        compiler_params=pltpu.CompilerParams(dimension_semantics=("arbitrary",), vmem_limit_bytes=VMEM_LIMIT),
        name="inproj",
    )(x2, nw, w_in, gmat, qw, kw)


def _attn_kernel(sink_ref, q_ref, k_ref, ks_ref, v_ref, vs_ref, kp_ref, ksp_ref, vp_ref, vsp_ref,
                 o_ref, kbuf, ksbuf, vbuf, vsbuf, sbuf):
    i = pl.program_id(1)
    for buf, prev, cur in ((kbuf, kp_ref, k_ref), (ksbuf, ksp_ref, ks_ref),
                           (vbuf, vp_ref, v_ref), (vsbuf, vsp_ref, vs_ref)):
        buf[0:WINDOW, :] = prev[0]
        buf[WINDOW:, :] = cur[0]

    rows2 = 2 * WINDOW
    r = lax.broadcasted_iota(jnp.int32, (rows2, WINDOW), 0) % WINDOW
    c = lax.broadcasted_iota(jnp.int32, (rows2, WINDOW), 1)
    upper = c > r
    top = lax.broadcasted_iota(jnp.int32, (rows2, 1), 0) < WINDOW
    lo = lax.broadcasted_iota(jnp.int32, (rows2, LANES), 1) < HEAD_DIM
    first_bias = jnp.where(i == 0, -jnp.inf, 0.0).astype(F32)

    def padded(b, own_buf, swapped_buf, hp, half):
        ls = slice(hp * LANES, (hp + 1) * LANES)
        band = slice(b * WINDOW, (b + 2) * WINDOW)
        own, oth = (own_buf, swapped_buf) if half == 0 else (swapped_buf, own_buf)
        zero = jnp.zeros((rows2, LANES), BF16)
        return jnp.where(lo, own[band, ls], zero), jnp.where(lo, zero, oth[band, ls])

    def scores(b):
        for h in range(N_KV_HEADS):
            col = h * GROUP * HEAD_DIM
            qrows = slice(b * WINDOW, (b + 1) * WINDOW)
            q2 = jnp.concatenate([q_ref[0, qrows, col:col + LANES], q_ref[0, qrows, col + LANES:col + 2 * LANES]],
                                 axis=0)
            for which, k_x in enumerate(padded(b, kbuf, ksbuf, h // 2, h % 2)):
                s = _dot_nt(q2, k_x)
                s_prev = s[:, :WINDOW] + first_bias if b == 0 else s[:, :WINDOW]
                sbuf[b % 2, 2 * h + which] = jnp.where(upper, s_prev, s[:, WINDOW:])

    def softmax_pv(b):
        for h in range(N_KV_HEADS):
            col = h * GROUP * HEAD_DIM
            out = jnp.zeros((rows2, LANES), F32)
            for which, v_x in enumerate(padded(b, vbuf, vsbuf, h // 2, h % 2)):
                sc = sbuf[b % 2, 2 * h + which]
                sink = jnp.where(top, sink_ref[h * GROUP + which], sink_ref[h * GROUP + 2 + which])
                m = jnp.maximum(jnp.max(sc, axis=-1, keepdims=True), sink)
                e = jnp.exp(sc - m)
                den = jnp.sum(e, axis=-1, keepdims=True) + jnp.exp(sink - m)
                p = e * (1.0 / den)
                pz = jnp.zeros_like(p)
                pcat = jnp.concatenate([jnp.where(upper, p, pz), jnp.where(upper, pz, p)], axis=1)
                out = out + _dot(pcat.astype(BF16), v_x)
            orows = slice(b * WINDOW, (b + 1) * WINDOW)
            o_ref[0, orows, col:col + LANES] = out[:WINDOW].astype(BF16)
            o_ref[0, orows, col + LANES:col + 2 * LANES] = out[WINDOW:].astype(BF16)

    nblk = TQ_ATTN // WINDOW
    scores(0)
    for b in range(nblk):
        if b + 1 < nblk:
            scores(b + 1)
        softmax_pv(b)


def _attention(sinks, q, k, ks, v, vs):
    b, s, _ = q.shape
    nprev = TQ_ATTN // WINDOW
    cur = lambda n: pl.BlockSpec((1, TQ_ATTN, n), lambda bi, i: (bi, i, 0))
    prev = pl.BlockSpec((1, WINDOW, KV_DIM), lambda bi, i: (bi, jnp.maximum(i * nprev - 1, 0), 0))
    kv_scratch = pltpu.VMEM((TQ_ATTN + WINDOW, KV_DIM), BF16)
    return pl.pallas_call(
        _attn_kernel,
        grid=(b, s // TQ_ATTN),
        in_specs=[pl.BlockSpec(memory_space=pltpu.SMEM), cur(Q_DIM), cur(KV_DIM), cur(KV_DIM), cur(KV_DIM),
                  cur(KV_DIM), prev, prev, prev, prev],
        out_specs=cur(Q_DIM),
        out_shape=jax.ShapeDtypeStruct((b, s, Q_DIM), BF16),
        scratch_shapes=[kv_scratch] * 4 + [pltpu.VMEM((2, 2 * N_KV_HEADS, 2 * WINDOW, WINDOW), F32)],
        compiler_params=pltpu.CompilerParams(dimension_semantics=("arbitrary", "arbitrary"),
                                             vmem_limit_bytes=VMEM_LIMIT),
        name="swa_attention",
    )(sinks, q, k, ks, v, vs, k, ks, v, vs)


def _conv_kernel(c_ref, cp_ref, w_ref, b_ref, lw_ref, lb_ref, o_ref, cbuf, sbuf, ybuf):
    i = pl.program_id(1)
    halo = cp_ref[0]
    cbuf[0:CONV_HALO, :] = jnp.where(i == 0, jnp.zeros_like(halo), halo)
    cbuf[CONV_HALO:, :] = c_ref[0]
    off = CONV_HALO - (CONV_K - 1)
    srows = sbuf.shape[1]

    for cb in range(CONV_CH // LANES):
        ls = slice(cb * LANES, (cb + 1) * LANES)
        for r in range(1, SUBLANES):
            sbuf[r - 1] = cbuf[r:r + srows, ls]

        def taps(rb, carry, ls=ls):
            r0 = pl.multiple_of(rb * CONV_ROWS, CONV_ROWS)
            acc = jnp.broadcast_to(b_ref[:, ls], (CONV_ROWS, LANES))
            for t in range(CONV_K):
                a, r = divmod(off + t, SUBLANES)
                if r == 0:
                    src = cbuf[pl.ds(r0 + a * SUBLANES, CONV_ROWS), ls]
                else:
                    src = sbuf[r - 1, pl.ds(r0 + a * SUBLANES, CONV_ROWS), :]
                acc = acc + w_ref[t:t + 1, ls] * src
            ybuf[pl.ds(r0, CONV_ROWS), ls] = acc
            return carry

        lax.fori_loop(0, TS_CONV // CONV_ROWS, taps, 0)

    def rows(rb, carry):
        r0 = pl.multiple_of(rb * CONV_ROWS, CONV_ROWS)
        y = ybuf[pl.ds(r0, CONV_ROWS), :]
        mu = jnp.mean(y, axis=-1, keepdims=True)
        d = y - mu
        var = jnp.mean(d * d, axis=-1, keepdims=True)
        z = d * lax.rsqrt(var + EPS) * lw_ref[...] + lb_ref[...]
        o_ref[0, pl.ds(r0, CONV_ROWS), :] = (z * jax.nn.sigmoid(z)).astype(BF16)
        return carry

    lax.fori_loop(0, TS_CONV // CONV_ROWS, rows, 0)


def _conv_module(c, w, bias, lw, lb):
    b, s, ch = c.shape
    nh = TS_CONV // CONV_HALO
    return pl.pallas_call(
        _conv_kernel,
        grid=(b, s // TS_CONV),
        in_specs=[pl.BlockSpec((1, TS_CONV, ch), lambda bi, i: (bi, i, 0)),
                  pl.BlockSpec((1, CONV_HALO, ch), lambda bi, i: (bi, jnp.maximum(i * nh - 1, 0), 0)),
                  _resident((CONV_K, ch)), _resident((1, ch)), _resident((1, ch)), _resident((1, ch))],
        out_specs=pl.BlockSpec((1, TS_CONV, ch), lambda bi, i: (bi, i, 0)),
        out_shape=jax.ShapeDtypeStruct((b, s, ch), BF16),
        scratch_shapes=[pltpu.VMEM((TS_CONV + CONV_HALO, ch), F32),
                        pltpu.VMEM((SUBLANES - 1, TS_CONV + CONV_HALO - SUBLANES, LANES), F32),
                        pltpu.VMEM((TS_CONV, ch), F32)],
        compiler_params=pltpu.CompilerParams(dimension_semantics=("arbitrary", "arbitrary"),
                                             vmem_limit_bytes=VMEM_LIMIT),
        name="conformer_conv",
    )(c, c, w, bias, lw, lb)


def _outproj_kernel(a_ref, c_ref, x_ref, w_ref, nw_ref, x1_ref, h_ref):
    x1 = x_ref[...] + _dot(a_ref[...], w_ref[0:Q_DIM, :]) + _dot(c_ref[...], w_ref[Q_DIM:, :])
    x1_ref[...] = x1
    ms = jnp.mean(x1 * x1, axis=-1, keepdims=True)
    h_ref[...] = (x1 * lax.rsqrt(ms + EPS) * nw_ref[...]).astype(BF16)


def _outproj(attn, conv, x2, w_out, nw):
    m = x2.shape[0]
    row = lambda n: pl.BlockSpec((TM_PROJ, n), lambda i: (i, 0))
    return pl.pallas_call(
        _outproj_kernel,
        grid=(m // TM_PROJ,),
        in_specs=[row(Q_DIM), row(CONV_CH), row(D_MODEL), _resident((D_MODEL, D_MODEL)), _resident((1, D_MODEL))],
        out_specs=[row(D_MODEL), row(D_MODEL)],
        out_shape=[jax.ShapeDtypeStruct((m, D_MODEL), F32), jax.ShapeDtypeStruct((m, D_MODEL), BF16)],
        compiler_params=pltpu.CompilerParams(dimension_semantics=("arbitrary",), vmem_limit_bytes=VMEM_LIMIT),
        name="outproj",
    )(attn, conv, x2, w_out, nw)


def _ffn_kernel(h_ref, x1_ref, wug_ref, wuv_ref, cwg_ref, cwv_ref, cbg_ref, cbv_ref, wd_ref, o_ref,
                gbuf, vbuf, carry, *, tiles_per_seq):
    i = pl.program_id(0)
    j = pl.program_id(1)
    hist = SUBLANES
    first_tile = i % tiles_per_seq == 0

    @pl.when(j == 0)
    def _():
        o_ref[...] = jnp.zeros_like(o_ref)

    nsub = FC_FFN // MXU_DIM
    sub_cols = [slice(sub * MXU_DIM, (sub + 1) * MXU_DIM) for sub in range(nsub)]

    def up(w_ref, buf, cols, slot):
        prev = carry[slot]
        buf[0:hist, :] = jnp.where(first_tile, jnp.zeros_like(prev), prev)
        buf[hist:, :] = _dot(h_ref[...], w_ref[:, cols])
        carry[slot] = buf[TM_FFN:TM_FFN + hist, :]

    def conv3(buf, cw_ref, cb_ref, cols):
        y = cb_ref[:, cols]
        for t in range(FFN_CONV_K):
            lag = FFN_CONV_K - 1 - t
            y = y + cw_ref[t:t + 1, cols] * buf[hist - lag:hist - lag + TM_FFN, :]
        return y

    for sub, cols in enumerate(sub_cols):
        up(wug_ref, gbuf.at[sub], cols, (2 * sub) * N_FC + j)
        up(wuv_ref, vbuf.at[sub], cols, (2 * sub + 1) * N_FC + j)
    for sub, cols in enumerate(sub_cols):
        gate = conv3(gbuf.at[sub], cwg_ref, cbg_ref, cols)
        val = conv3(vbuf.at[sub], cwv_ref, cbv_ref, cols)
        act = (gate * jax.nn.sigmoid(gate) * val).astype(BF16)
        for nb in range(D_MODEL // MXU_DIM):
            ncols = slice(nb * MXU_DIM, (nb + 1) * MXU_DIM)
            o_ref[:, ncols] += _dot(act, wd_ref[cols, ncols])

    @pl.when(j < X1_SLABS)
    def _():
        rows = pl.ds(pl.multiple_of(j * (TM_FFN // X1_SLABS), TM_FFN // X1_SLABS), TM_FFN // X1_SLABS)
        o_ref[rows, :] += x1_ref[...]


def _ffn(h2, x1, wu, cw, cb, wd, tiles_per_seq):
    m = h2.shape[0]
    row = lambda n: pl.BlockSpec((TM_FFN, n), lambda i, j: (i, 0))
    gate_cols = lambda r: pl.BlockSpec((r, FC_FFN), lambda i, j: (0, j))
    val_cols = lambda r: pl.BlockSpec((r, FC_FFN), lambda i, j: (0, N_FC + j))
    slab = TM_FFN // X1_SLABS
    return pl.pallas_call(
        functools.partial(_ffn_kernel, tiles_per_seq=tiles_per_seq),
        grid=(m // TM_FFN, N_FC),
        in_specs=[row(D_MODEL),
                  pl.BlockSpec((slab, D_MODEL), lambda i, j: (i * X1_SLABS + jnp.minimum(j, X1_SLABS - 1), 0)),
                  gate_cols(D_MODEL), val_cols(D_MODEL), gate_cols(FFN_CONV_K), val_cols(FFN_CONV_K),
                  gate_cols(1), val_cols(1),
                  pl.BlockSpec((FC_FFN, D_MODEL), lambda i, j: (j, 0))],
        out_specs=row(D_MODEL),
        out_shape=jax.ShapeDtypeStruct((m, D_MODEL), F32),
        scratch_shapes=[pltpu.VMEM((FC_FFN // MXU_DIM, TM_FFN + SUBLANES, MXU_DIM), F32),
                        pltpu.VMEM((FC_FFN // MXU_DIM, TM_FFN + SUBLANES, MXU_DIM), F32),
                        pltpu.VMEM((2 * (FC_FFN // MXU_DIM) * N_FC, SUBLANES, MXU_DIM), F32)],
        compiler_params=pltpu.CompilerParams(dimension_semantics=("arbitrary", "arbitrary"),
                                             vmem_limit_bytes=VMEM_LIMIT),
        name="conv_ffn",
    )(h2, x1, wu, wu, cw, cw, cb, cb, wd)


def _layer(x, attn_norm_w, w_in, q_norm_w, k_norm_w, sinks, conv_dw_w, conv_dw_b, conv_ln_w, conv_ln_b,
           w_out, ffn_norm_w, w_up, ffn_dw_w, ffn_dw_b, w_down):
    b, s, d = x.shape
    m = b * s
    x2 = x.reshape(m, d)
    head = lax.broadcasted_iota(jnp.int32, (MXU_DIM, MXU_DIM), 0) // HEAD_DIM
    gmat = jnp.where(head == head.T, 1.0 / HEAD_DIM, 0.0).astype(BF16)
    qw = jnp.tile(q_norm_w, MXU_DIM // HEAD_DIM)[None, :]
    kw = jnp.tile(k_norm_w, KV_DIM // HEAD_DIM)[None, :]
    q, k, ks, v, vs, c = _inproj(x2, attn_norm_w[None, :], w_in.astype(BF16), gmat, qw, kw)
    seq = lambda t: t.reshape(b, s, t.shape[-1])
    attn = _attention(sinks, seq(q), seq(k), seq(ks), seq(v), seq(vs))
    conv = _conv_module(seq(c), conv_dw_w, conv_dw_b[None, :], conv_ln_w[None, :], conv_ln_b[None, :])
    x1, h2 = _outproj(attn.reshape(m, Q_DIM), conv.reshape(m, CONV_CH), x2, w_out.astype(BF16),
                      ffn_norm_w[None, :])
    out = _ffn(h2, x1, w_up.astype(BF16), ffn_dw_w, ffn_dw_b[None, :], w_down.astype(BF16), s // TM_FFN)
    return out.reshape(b, s, d)


def kernel(x, attn_norm_w, w_in, q_norm_w, k_norm_w, sinks, conv_dw_w, conv_dw_b, conv_ln_w, conv_ln_b, w_out,
           ffn_norm_w, w_up, ffn_dw_w, ffn_dw_b, w_down):
    for l in range(attn_norm_w.shape[0]):
        x = _layer(x, attn_norm_w[l], w_in[l], q_norm_w[l], k_norm_w[l], sinks[l], conv_dw_w[l], conv_dw_b[l],
                   conv_ln_w[l], conv_ln_b[l], w_out[l], ffn_norm_w[l], w_up[l], ffn_dw_w[l], ffn_dw_b[l],
                   w_down[l])
    return x
```

```python
import functools

import jax
import jax.numpy as jnp
from jax import lax
from jax.experimental import pallas as pl
from jax.experimental.pallas import tpu as pltpu

F32 = jnp.float32
BF16 = jnp.bfloat16

D_MODEL = 2048
N_Q_HEADS = 16
N_KV_HEADS = 4
HEAD_DIM = 64
GROUP = N_Q_HEADS // N_KV_HEADS
Q_DIM = N_Q_HEADS * HEAD_DIM
KV_DIM = N_KV_HEADS * HEAD_DIM
WINDOW = 128
CONV_CH = D_MODEL - Q_DIM
CONV_K = 31
D_FF = 5632
FFN_CONV_K = 3
EPS = 1e-6

LANES = 128
SUBLANES = 8
MXU_DIM = 256
VMEM_LIMIT = 62 * 1024 * 1024

TM_PROJ = 512
TQ_ATTN = 512
TS_CONV = 512
CONV_HALO = 32
CONV_ROWS = 64
TM_FFN = 1024
X1_SLABS = 8
FC_FFN = 512
N_FC = D_FF // FC_FFN


def _dot(a, b):
    return jnp.dot(a, b, preferred_element_type=F32)


def _dot_nt(a, b):
    return lax.dot_general(a, b, (((1,), (1,)), ((), ())), preferred_element_type=F32)


def _resident(shape):
    return pl.BlockSpec(shape, lambda *_: (0,) * len(shape), pipeline_mode=pl.Buffered(1))


def _inproj_kernel(x_ref, nw_ref, w_ref, g_ref, qw_ref, kw_ref, cw_ref, cb_ref, lw_ref, lb_ref,
                   q_ref, k_ref, ks_ref, v_ref, vs_ref, c_ref, cbuf, ybuf):
    i = pl.program_id(1)
    x = x_ref[0]
    ms = jnp.mean(x * x, axis=-1, keepdims=True)
    h = (x * lax.rsqrt(ms + EPS) * nw_ref[...]).astype(BF16)
    g = g_ref[...]

    def head_norm(t, w):
        sq = t * t
        hi = sq.astype(BF16)
        lo = (sq - hi.astype(F32)).astype(BF16)
        hms = _dot(hi, g) + _dot(lo, g)
        return t * lax.rsqrt(hms + EPS) * w

    def swap_halves(t):
        return jnp.concatenate(
            [pltpu.roll(t[:, i * LANES:(i + 1) * LANES], HEAD_DIM, axis=1)
             for i in range(t.shape[1] // LANES)], axis=1)

    a0 = Q_DIM + 2 * KV_DIM
    g0 = a0 + CONV_CH
    nparts = CONV_CH // MXU_DIM
    ncol = CONV_CH // LANES

    @pl.when(i == 0)
    def _():
        cbuf[:, 0:CONV_HALO, :] = jnp.zeros((ncol, CONV_HALO, LANES), F32)

    def glu(part):
        a = _dot(h, w_ref[:, a0 + part * MXU_DIM:a0 + (part + 1) * MXU_DIM])
        gt = _dot(h, w_ref[:, g0 + part * MXU_DIM:g0 + (part + 1) * MXU_DIM])
        c = a * jax.nn.sigmoid(gt)
        for lc in range(MXU_DIM // LANES):
            cbuf[part * (MXU_DIM // LANES) + lc, CONV_HALO:, :] = c[:, lc * LANES:(lc + 1) * LANES]

    off = CONV_HALO - (CONV_K - 1)

    def zero_after(t):
        def z(v):
            bits = pltpu.bitcast(v, jnp.uint32)
            return ((bits >> 16) >> 16).astype(F32)
        return z(t[0:SUBLANES, 0:LANES])[0:1] + z(t[-SUBLANES:, -LANES:])[0:1]

    def conv_column(cb, after=None):
        ls = slice(cb * LANES, (cb + 1) * LANES)
        bias = cb_ref[:, ls] if after is None else cb_ref[:, ls] + zero_after(after)
        for rb in range(TS_CONV // CONV_ROWS):
            r0 = rb * CONV_ROWS
            acc = jnp.broadcast_to(bias, (CONV_ROWS, LANES))
            for t in range(CONV_K):
                acc = acc + cw_ref[t:t + 1, ls] * cbuf[cb, r0 + off + t:r0 + off + t + CONV_ROWS, :]
            ybuf[r0:r0 + CONV_ROWS, ls] = acc

    for part in range(nparts):
        glu(part)

    scale = 1.0 / (HEAD_DIM ** 0.5)
    wide = 2 * MXU_DIM
    proj = [_dot(h, w_ref[:, n * wide:(n + 1) * wide]) for n in range((Q_DIM + 2 * KV_DIM) // wide)]
    for n in range(Q_DIM // wide):
        for j in range(wide // MXU_DIM):
            sl = slice(j * MXU_DIM, (j + 1) * MXU_DIM)
            qn = head_norm(proj[n][:, sl], qw_ref[...]) * scale
            q_ref[0, :, n * wide + j * MXU_DIM:n * wide + (j + 1) * MXU_DIM] = qn.astype(BF16)
    kv = proj[Q_DIM // wide]
    k = head_norm(kv[:, 0:KV_DIM], kw_ref[...])
    k_ref[0] = k.astype(BF16)
    ks_ref[0] = swap_halves(k).astype(BF16)
    v = kv[:, KV_DIM:2 * KV_DIM]
    v_ref[0] = v.astype(BF16)
    vs_ref[0] = swap_halves(v).astype(BF16)

    for cb in range(ncol):
        n = cb - (ncol - 2)
        conv_column(cb, proj[n] if n >= 0 else None)

    for rb in range(TS_CONV // CONV_ROWS):
        rows = slice(rb * CONV_ROWS, (rb + 1) * CONV_ROWS)
        y = ybuf[rows, :]
        mu = jnp.mean(y, axis=-1, keepdims=True)
        d = y - mu
        var = jnp.mean(d * d, axis=-1, keepdims=True)
        z = d * lax.rsqrt(var + EPS) * lw_ref[...] + lb_ref[...]
        c_ref[0, rows, :] = (z * jax.nn.sigmoid(z)).astype(BF16)

    cbuf[:, 0:CONV_HALO, :] = cbuf[:, TS_CONV:TS_CONV + CONV_HALO, :]


def _inproj(x, nw, w_in, gmat, qw, kw, cw, cb, lw, lb):
    b, s, _ = x.shape
    in_cols = w_in.shape[1]
    row = lambda n: pl.BlockSpec((1, TS_CONV, n), lambda bi, i: (bi, i, 0))
    return pl.pallas_call(
        _inproj_kernel,
        grid=(b, s // TS_CONV),
        in_specs=[row(D_MODEL), _resident((1, D_MODEL)), _resident((D_MODEL, in_cols)),
                  _resident((MXU_DIM, MXU_DIM)), _resident((1, MXU_DIM)), _resident((1, KV_DIM)),
                  _resident((CONV_K, CONV_CH)), _resident((1, CONV_CH)), _resident((1, CONV_CH)),
                  _resident((1, CONV_CH))],
        out_specs=[row(Q_DIM), row(KV_DIM), row(KV_DIM), row(KV_DIM), row(KV_DIM), row(CONV_CH)],
        out_shape=[jax.ShapeDtypeStruct((b, s, Q_DIM), BF16)] + [jax.ShapeDtypeStruct((b, s, KV_DIM), BF16)] * 4
        + [jax.ShapeDtypeStruct((b, s, CONV_CH), BF16)],
        scratch_shapes=[pltpu.VMEM((CONV_CH // LANES, TS_CONV + CONV_HALO, LANES), F32),
                        pltpu.VMEM((TS_CONV, CONV_CH), F32)],
        compiler_params=pltpu.CompilerParams(dimension_semantics=("arbitrary", "arbitrary"),
                                             vmem_limit_bytes=VMEM_LIMIT),
        name="inproj_conv",
    )(x, nw, w_in, gmat, qw, kw, cw, cb, lw, lb)


def _attn_kernel(sink_ref, q_ref, k_ref, ks_ref, v_ref, vs_ref, kp_ref, ksp_ref, vp_ref, vsp_ref,
                 o_ref, kbuf, ksbuf, vbuf, vsbuf, sbuf):
    i = pl.program_id(1)
    for buf, prev, cur in ((kbuf, kp_ref, k_ref), (ksbuf, ksp_ref, ks_ref),
                           (vbuf, vp_ref, v_ref), (vsbuf, vsp_ref, vs_ref)):
        buf[0:WINDOW, :] = prev[0]
        buf[WINDOW:, :] = cur[0]

    rows2 = 2 * WINDOW
    r = lax.broadcasted_iota(jnp.int32, (rows2, WINDOW), 0) % WINDOW
    c = lax.broadcasted_iota(jnp.int32, (rows2, WINDOW), 1)
    upper = c > r
    top = lax.broadcasted_iota(jnp.int32, (rows2, 1), 0) < WINDOW
    lo = lax.broadcasted_iota(jnp.int32, (rows2, LANES), 1) < HEAD_DIM
    first_bias = jnp.where(i == 0, -jnp.inf, 0.0).astype(F32)

    def padded(b, own_buf, swapped_buf, hp, half):
        ls = slice(hp * LANES, (hp + 1) * LANES)
        band = slice(b * WINDOW, (b + 2) * WINDOW)
        own, oth = (own_buf, swapped_buf) if half == 0 else (swapped_buf, own_buf)
        zero = jnp.zeros((rows2, LANES), BF16)
        return jnp.where(lo, own[band, ls], zero), jnp.where(lo, zero, oth[band, ls])

    def scores(b):
        for h in range(N_KV_HEADS):
            col = h * GROUP * HEAD_DIM
            qrows = slice(b * WINDOW, (b + 1) * WINDOW)
            q2 = jnp.concatenate([q_ref[0, qrows, col:col + LANES], q_ref[0, qrows, col + LANES:col + 2 * LANES]],
                                 axis=0)
            for which, k_x in enumerate(padded(b, kbuf, ksbuf, h // 2, h % 2)):
                s = _dot_nt(q2, k_x)
                s_prev = s[:, :WINDOW] + first_bias if b == 0 else s[:, :WINDOW]
                sbuf[b % 2, 2 * h + which] = jnp.where(upper, s_prev, s[:, WINDOW:])

    def softmax_pv(b):
        for h in range(N_KV_HEADS):
            col = h * GROUP * HEAD_DIM
            out = jnp.zeros((rows2, LANES), F32)
            for which, v_x in enumerate(padded(b, vbuf, vsbuf, h // 2, h % 2)):
                sc = sbuf[b % 2, 2 * h + which]
                sink = jnp.where(top, sink_ref[h * GROUP + which], sink_ref[h * GROUP + 2 + which])
                m = jnp.maximum(jnp.max(sc, axis=-1, keepdims=True), sink)
                e = jnp.exp(sc - m)
                den = jnp.sum(e, axis=-1, keepdims=True) + jnp.exp(sink - m)
                p = e * (1.0 / den)
                pz = jnp.zeros_like(p)
                pcat = jnp.concatenate([jnp.where(upper, p, pz), jnp.where(upper, pz, p)], axis=1)
                out = out + _dot(pcat.astype(BF16), v_x)
            orows = slice(b * WINDOW, (b + 1) * WINDOW)
            o_ref[0, orows, col:col + LANES] = out[:WINDOW].astype(BF16)
            o_ref[0, orows, col + LANES:col + 2 * LANES] = out[WINDOW:].astype(BF16)

    nblk = TQ_ATTN // WINDOW
    scores(0)
    for b in range(nblk):
        if b + 1 < nblk:
            scores(b + 1)
        softmax_pv(b)


def _attention(sinks, q, k, ks, v, vs):
    b, s, _ = q.shape
    nprev = TQ_ATTN // WINDOW
    cur = lambda n: pl.BlockSpec((1, TQ_ATTN, n), lambda bi, i: (bi, i, 0))
    prev = pl.BlockSpec((1, WINDOW, KV_DIM), lambda bi, i: (bi, jnp.maximum(i * nprev - 1, 0), 0))
    kv_scratch = pltpu.VMEM((TQ_ATTN + WINDOW, KV_DIM), BF16)
    return pl.pallas_call(
        _attn_kernel,
        grid=(b, s // TQ_ATTN),
        in_specs=[pl.BlockSpec(memory_space=pltpu.SMEM), cur(Q_DIM), cur(KV_DIM), cur(KV_DIM), cur(KV_DIM),
                  cur(KV_DIM), prev, prev, prev, prev],
        out_specs=cur(Q_DIM),
        out_shape=jax.ShapeDtypeStruct((b, s, Q_DIM), BF16),
        scratch_shapes=[kv_scratch] * 4 + [pltpu.VMEM((2, 2 * N_KV_HEADS, 2 * WINDOW, WINDOW), F32)],
        compiler_params=pltpu.CompilerParams(dimension_semantics=("arbitrary", "arbitrary"),
                                             vmem_limit_bytes=VMEM_LIMIT),
        name="swa_attention",
    )(sinks, q, k, ks, v, vs, k, ks, v, vs)


def _outproj_kernel(a_ref, c_ref, x_ref, w_ref, nw_ref, x1_ref, h_ref):
    x1 = x_ref[...] + _dot(a_ref[...], w_ref[0:Q_DIM, :]) + _dot(c_ref[...], w_ref[Q_DIM:, :])
    x1_ref[...] = x1
    ms = jnp.mean(x1 * x1, axis=-1, keepdims=True)
    h_ref[...] = (x1 * lax.rsqrt(ms + EPS) * nw_ref[...]).astype(BF16)


def _outproj(attn, conv, x2, w_out, nw):
    m = x2.shape[0]
    row = lambda n: pl.BlockSpec((TM_PROJ, n), lambda i: (i, 0))
    return pl.pallas_call(
        _outproj_kernel,
        grid=(m // TM_PROJ,),
        in_specs=[row(Q_DIM), row(CONV_CH), row(D_MODEL), _resident((D_MODEL, D_MODEL)), _resident((1, D_MODEL))],
        out_specs=[row(D_MODEL), row(D_MODEL)],
        out_shape=[jax.ShapeDtypeStruct((m, D_MODEL), F32), jax.ShapeDtypeStruct((m, D_MODEL), BF16)],
        compiler_params=pltpu.CompilerParams(dimension_semantics=("arbitrary",), vmem_limit_bytes=VMEM_LIMIT),
        name="outproj",
    )(attn, conv, x2, w_out, nw)


def _ffn_kernel(h_ref, x1_ref, wug_ref, wuv_ref, cwg_ref, cwv_ref, cbg_ref, cbv_ref, wd_ref, o_ref,
                gbuf, vbuf, carry, *, tiles_per_seq):
    i = pl.program_id(0)
    j = pl.program_id(1)
    hist = SUBLANES
    first_tile = i % tiles_per_seq == 0
    nsub = FC_FFN // MXU_DIM
    sub_cols = [slice(sub * MXU_DIM, (sub + 1) * MXU_DIM) for sub in range(nsub)]

    lane_cols = MXU_DIM // LANES

    def up(chunk):
        for sub, cols in enumerate(sub_cols):
            for kind, (w_ref, bufs) in enumerate(((wug_ref, gbuf), (wuv_ref, vbuf))):
                u = _dot(h_ref[...], w_ref[:, cols])
                for lc in range(lane_cols):
                    c = sub * lane_cols + lc
                    slot = (2 * c + kind) * N_FC + chunk
                    prev = carry[slot]
                    bufs[c, 0:hist, :] = jnp.where(first_tile, jnp.zeros_like(prev), prev)
                    bufs[c, hist:, :] = u[:, lc * LANES:(lc + 1) * LANES]
                    carry[slot] = bufs[c, TM_FFN:TM_FFN + hist, :]

    def conv3(bufs, sub, cw_ref, cb_ref, cols):
        ys = []
        for lc in range(lane_cols):
            c = sub * lane_cols + lc
            ls = slice(cols.start + lc * LANES, cols.start + (lc + 1) * LANES)
            y = cb_ref[:, ls]
            for t in range(FFN_CONV_K):
                lag = FFN_CONV_K - 1 - t
                y = y + cw_ref[t:t + 1, ls] * bufs[c, hist - lag:hist - lag + TM_FFN, :]
            ys.append(y)
        return jnp.concatenate(ys, axis=1)

    def activate_down():
        for sub, cols in enumerate(sub_cols):
            gate = conv3(gbuf, sub, cwg_ref, cbg_ref, cols)
            val = conv3(vbuf, sub, cwv_ref, cbv_ref, cols)
            act = (gate * jax.nn.sigmoid(gate) * val).astype(BF16)
            for nb in range(D_MODEL // MXU_DIM):
                ncols = slice(nb * MXU_DIM, (nb + 1) * MXU_DIM)
                o_ref[:, ncols] += _dot(act, wd_ref[cols, ncols])

    @pl.when(j == 0)
    def _():
        o_ref[...] = jnp.zeros_like(o_ref)

    up(j)
    activate_down()

    @pl.when(j < X1_SLABS)
    def _():
        rows = pl.ds(pl.multiple_of(j * (TM_FFN // X1_SLABS), TM_FFN // X1_SLABS), TM_FFN // X1_SLABS)
        o_ref[rows, :] += x1_ref[...]


def _ffn(h2, x1, wu, cw, cb, wd, tiles_per_seq):
    m = h2.shape[0]
    slab = TM_FFN // X1_SLABS
    up_chunk = act_chunk = lambda j: j
    return pl.pallas_call(
        functools.partial(_ffn_kernel, tiles_per_seq=tiles_per_seq),
        grid=(m // TM_FFN, N_FC),
        in_specs=[pl.BlockSpec((TM_FFN, D_MODEL), lambda i, j: (i, 0)),
                  pl.BlockSpec((slab, D_MODEL), lambda i, j: (i * X1_SLABS + jnp.minimum(j, X1_SLABS - 1), 0)),
                  pl.BlockSpec((D_MODEL, FC_FFN), lambda i, j: (0, up_chunk(j))),
                  pl.BlockSpec((D_MODEL, FC_FFN), lambda i, j: (0, N_FC + up_chunk(j))),
                  pl.BlockSpec((FFN_CONV_K, FC_FFN), lambda i, j: (0, act_chunk(j))),
                  pl.BlockSpec((FFN_CONV_K, FC_FFN), lambda i, j: (0, N_FC + act_chunk(j))),
                  pl.BlockSpec((1, FC_FFN), lambda i, j: (0, act_chunk(j))),
                  pl.BlockSpec((1, FC_FFN), lambda i, j: (0, N_FC + act_chunk(j))),
                  pl.BlockSpec((FC_FFN, D_MODEL), lambda i, j: (act_chunk(j), 0))],
        out_specs=pl.BlockSpec((TM_FFN, D_MODEL), lambda i, j: (i, 0)),
        out_shape=jax.ShapeDtypeStruct((m, D_MODEL), F32),
        scratch_shapes=[pltpu.VMEM((FC_FFN // LANES, TM_FFN + SUBLANES, LANES), F32),
                        pltpu.VMEM((FC_FFN // LANES, TM_FFN + SUBLANES, LANES), F32),
                        pltpu.VMEM((2 * (FC_FFN // LANES) * N_FC, SUBLANES, LANES), F32)],
        compiler_params=pltpu.CompilerParams(dimension_semantics=("arbitrary", "arbitrary"),
                                             vmem_limit_bytes=VMEM_LIMIT),
        name="conv_ffn",
    )(h2, x1, wu, wu, cw, cw, cb, cb, wd)


def _layer(x, attn_norm_w, w_in, q_norm_w, k_norm_w, sinks, conv_dw_w, conv_dw_b, conv_ln_w, conv_ln_b,
           w_out, ffn_norm_w, w_up, ffn_dw_w, ffn_dw_b, w_down):
    b, s, d = x.shape
    m = b * s
    x2 = x.reshape(m, d)
    head = lax.broadcasted_iota(jnp.int32, (MXU_DIM, MXU_DIM), 0) // HEAD_DIM
    gmat = jnp.where(head == head.T, 1.0 / HEAD_DIM, 0.0).astype(BF16)
    qw = jnp.tile(q_norm_w, MXU_DIM // HEAD_DIM)[None, :]
    kw = jnp.tile(k_norm_w, KV_DIM // HEAD_DIM)[None, :]
    q, k, ks, v, vs, conv = _inproj(x, attn_norm_w[None, :], w_in.astype(BF16), gmat, qw, kw, conv_dw_w,
                                    conv_dw_b[None, :], conv_ln_w[None, :], conv_ln_b[None, :])
    attn = _attention(sinks, q, k, ks, v, vs)
    x1, h2 = _outproj(attn.reshape(m, Q_DIM), conv.reshape(m, CONV_CH), x2, w_out.astype(BF16),
                      ffn_norm_w[None, :])
    out = _ffn(h2, x1, w_up.astype(BF16), ffn_dw_w, ffn_dw_b[None, :], w_down.astype(BF16), s // TM_FFN)
    return out.reshape(b, s, d)


def kernel(x, attn_norm_w, w_in, q_norm_w, k_norm_w, sinks, conv_dw_w, conv_dw_b, conv_ln_w, conv_ln_b, w_out,
           ffn_norm_w, w_up, ffn_dw_w, ffn_dw_b, w_down):
    for l in range(attn_norm_w.shape[0]):
        x = _layer(x, attn_norm_w[l], w_in[l], q_norm_w[l], k_norm_w[l], sinks[l], conv_dw_w[l], conv_dw_b[l],
                   conv_ln_w[l], conv_ln_b[l], w_out[l], ffn_norm_w[l], w_up[l], ffn_dw_w[l], ffn_dw_b[l],
                   w_down[l])
    return x
```

```python
import functools

import jax
import jax.numpy as jnp
from jax import lax
from jax.experimental import pallas as pl
from jax.experimental.pallas import tpu as pltpu

F32 = jnp.float32
BF16 = jnp.bfloat16

D_MODEL = 2048
N_Q_HEADS = 16
N_KV_HEADS = 4
HEAD_DIM = 64
GROUP = N_Q_HEADS // N_KV_HEADS
Q_DIM = N_Q_HEADS * HEAD_DIM
KV_DIM = N_KV_HEADS * HEAD_DIM
WINDOW = 128
CONV_CH = D_MODEL - Q_DIM
CONV_K = 31
D_FF = 5632
FFN_CONV_K = 3
EPS = 1e-6

LANES = 128
SUBLANES = 8
MXU_DIM = 256
VMEM_LIMIT = 62 * 1024 * 1024

TQ_ATTN = 512
TS_CONV = 512
CONV_HALO = 32
CONV_ROWS = 64
TM_FFN = 1024
X1_SLABS = 8
FC_FFN = 512
N_FC = D_FF // FC_FFN
DOWN_COLS = 512


def _dot(a, b):
    return jnp.dot(a, b, preferred_element_type=F32)


def _dot_nt(a, b):
    return lax.dot_general(a, b, (((1,), (1,)), ((), ())), preferred_element_type=F32)


def _resident(shape):
    return pl.BlockSpec(shape, lambda *_: (0,) * len(shape), pipeline_mode=pl.Buffered(1))


def _inproj_kernel(x_ref, nw_ref, w_ref, g_ref, qw_ref, kw_ref, cw_ref, cb_ref, lw_ref, lb_ref,
                   q_ref, k_ref, ks_ref, v_ref, vs_ref, c_ref, cbuf, ybuf):
    i = pl.program_id(1)
    x = x_ref[0]
    ms = jnp.mean(x * x, axis=-1, keepdims=True)
    h = (x * lax.rsqrt(ms + EPS) * nw_ref[...]).astype(BF16)
    g = g_ref[...]

    def head_norm(t, w):
        sq = t * t
        hi = sq.astype(BF16)
        lo = (sq - hi.astype(F32)).astype(BF16)
        hms = _dot(hi, g) + _dot(lo, g)
        return t * lax.rsqrt(hms + EPS) * w

    def swap_halves(t):
        return jnp.concatenate(
            [pltpu.roll(t[:, i * LANES:(i + 1) * LANES], HEAD_DIM, axis=1)
             for i in range(t.shape[1] // LANES)], axis=1)

    a0 = Q_DIM + 2 * KV_DIM
    g0 = a0 + CONV_CH
    nparts = CONV_CH // MXU_DIM
    ncol = CONV_CH // LANES

    @pl.when(i == 0)
    def _():
        cbuf[:, 0:CONV_HALO, :] = jnp.zeros((ncol, CONV_HALO, LANES), F32)

    def glu(part):
        a = _dot(h, w_ref[:, a0 + part * MXU_DIM:a0 + (part + 1) * MXU_DIM])
        gt = _dot(h, w_ref[:, g0 + part * MXU_DIM:g0 + (part + 1) * MXU_DIM])
        c = a * jax.nn.sigmoid(gt)
        for lc in range(MXU_DIM // LANES):
            cbuf[part * (MXU_DIM // LANES) + lc, CONV_HALO:, :] = c[:, lc * LANES:(lc + 1) * LANES]

    off = CONV_HALO - (CONV_K - 1)

    def zero_after(t):
        def z(v):
            bits = pltpu.bitcast(v, jnp.uint32)
            return ((bits >> 16) >> 16).astype(F32)
        return z(t[0:SUBLANES, 0:LANES])[0:1] + z(t[-SUBLANES:, -LANES:])[0:1]

    def conv_column(cb, after=None):
        ls = slice(cb * LANES, (cb + 1) * LANES)
        bias = cb_ref[:, ls] if after is None else cb_ref[:, ls] + zero_after(after)
        for rb in range(TS_CONV // CONV_ROWS):
            r0 = rb * CONV_ROWS
            acc = jnp.broadcast_to(bias, (CONV_ROWS, LANES))
            for t in range(CONV_K):
                acc = acc + cw_ref[t:t + 1, ls] * cbuf[cb, r0 + off + t:r0 + off + t + CONV_ROWS, :]
            ybuf[r0:r0 + CONV_ROWS, ls] = acc

    for part in range(nparts):
        glu(part)

    scale = 1.0 / (HEAD_DIM ** 0.5)
    wide = 2 * MXU_DIM
    proj = [_dot(h, w_ref[:, n * wide:(n + 1) * wide]) for n in range((Q_DIM + 2 * KV_DIM) // wide)]
    for n in range(Q_DIM // wide):
        for j in range(wide // MXU_DIM):
            sl = slice(j * MXU_DIM, (j + 1) * MXU_DIM)
            qn = head_norm(proj[n][:, sl], qw_ref[...]) * scale
            q_ref[0, :, n * wide + j * MXU_DIM:n * wide + (j + 1) * MXU_DIM] = qn.astype(BF16)
    kv = proj[Q_DIM // wide]
    k = head_norm(kv[:, 0:KV_DIM], kw_ref[...])
    k_ref[0] = k.astype(BF16)
    ks_ref[0] = swap_halves(k).astype(BF16)
    v = kv[:, KV_DIM:2 * KV_DIM]
    v_ref[0] = v.astype(BF16)
    vs_ref[0] = swap_halves(v).astype(BF16)

    for cb in range(ncol):
        n = cb - (ncol - 2)
        conv_column(cb, proj[n] if n >= 0 else None)

    for rb in range(TS_CONV // CONV_ROWS):
        rows = slice(rb * CONV_ROWS, (rb + 1) * CONV_ROWS)
        y = ybuf[rows, :]
        mu = jnp.mean(y, axis=-1, keepdims=True)
        d = y - mu
        var = jnp.mean(d * d, axis=-1, keepdims=True)
        z = d * lax.rsqrt(var + EPS) * lw_ref[...] + lb_ref[...]
        c_ref[0, rows, :] = (z * jax.nn.sigmoid(z)).astype(BF16)

    cbuf[:, 0:CONV_HALO, :] = cbuf[:, TS_CONV:TS_CONV + CONV_HALO, :]


def _inproj(x, nw, w_in, gmat, qw, kw, cw, cb, lw, lb):
    b, s, _ = x.shape
    in_cols = w_in.shape[1]
    row = lambda n: pl.BlockSpec((1, TS_CONV, n), lambda bi, i: (bi, i, 0))
    return pl.pallas_call(
        _inproj_kernel,
        grid=(b, s // TS_CONV),
        in_specs=[row(D_MODEL), _resident((1, D_MODEL)), _resident((D_MODEL, in_cols)),
                  _resident((MXU_DIM, MXU_DIM)), _resident((1, MXU_DIM)), _resident((1, KV_DIM)),
                  _resident((CONV_K, CONV_CH)), _resident((1, CONV_CH)), _resident((1, CONV_CH)),
                  _resident((1, CONV_CH))],
        out_specs=[row(Q_DIM), row(KV_DIM), row(KV_DIM), row(KV_DIM), row(KV_DIM), row(CONV_CH)],
        out_shape=[jax.ShapeDtypeStruct((b, s, Q_DIM), BF16)] + [jax.ShapeDtypeStruct((b, s, KV_DIM), BF16)] * 4
        + [jax.ShapeDtypeStruct((b, s, CONV_CH), BF16)],
        scratch_shapes=[pltpu.VMEM((CONV_CH // LANES, TS_CONV + CONV_HALO, LANES), F32),
                        pltpu.VMEM((TS_CONV, CONV_CH), F32)],
        compiler_params=pltpu.CompilerParams(dimension_semantics=("arbitrary", "arbitrary"),
                                             vmem_limit_bytes=VMEM_LIMIT),
        name="inproj_conv",
    )(x, nw, w_in, gmat, qw, kw, cw, cb, lw, lb)


def _attn_kernel(sink_ref, q_ref, k_ref, ks_ref, v_ref, vs_ref, kp_ref, ksp_ref, vp_ref, vsp_ref,
                 c_ref, x_ref, w_ref, nw_ref, x1_ref, h_ref, kbuf, ksbuf, vbuf, vsbuf, sbuf, abuf):
    i = pl.program_id(1)
    for buf, prev, cur in ((kbuf, kp_ref, k_ref), (ksbuf, ksp_ref, ks_ref),
                           (vbuf, vp_ref, v_ref), (vsbuf, vsp_ref, vs_ref)):
        buf[0:WINDOW, :] = prev[0]
        buf[WINDOW:, :] = cur[0]

    rows2 = 2 * WINDOW
    r = lax.broadcasted_iota(jnp.int32, (rows2, WINDOW), 0) % WINDOW
    c = lax.broadcasted_iota(jnp.int32, (rows2, WINDOW), 1)
    upper = c > r
    top = lax.broadcasted_iota(jnp.int32, (rows2, 1), 0) < WINDOW
    lo = lax.broadcasted_iota(jnp.int32, (rows2, LANES), 1) < HEAD_DIM
    first_bias = jnp.where(i == 0, -jnp.inf, 0.0).astype(F32)

    def padded(b, own_buf, swapped_buf, hp, half):
        ls = slice(hp * LANES, (hp + 1) * LANES)
        band = slice(b * WINDOW, (b + 2) * WINDOW)
        own, oth = (own_buf, swapped_buf) if half == 0 else (swapped_buf, own_buf)
        zero = jnp.zeros((rows2, LANES), BF16)
        return jnp.where(lo, own[band, ls], zero), jnp.where(lo, zero, oth[band, ls])

    def scores(b):
        for h in range(N_KV_HEADS):
            col = h * GROUP * HEAD_DIM
            qrows = slice(b * WINDOW, (b + 1) * WINDOW)
            q2 = jnp.concatenate([q_ref[0, qrows, col:col + LANES], q_ref[0, qrows, col + LANES:col + 2 * LANES]],
                                 axis=0)
            for which, k_x in enumerate(padded(b, kbuf, ksbuf, h // 2, h % 2)):
                s = _dot_nt(q2, k_x)
                s_prev = s[:, :WINDOW] + first_bias if b == 0 else s[:, :WINDOW]
                sbuf[b % 2, 2 * h + which] = jnp.where(upper, s_prev, s[:, WINDOW:])

    def softmax_pv(b):
        for h in range(N_KV_HEADS):
            col = h * GROUP * HEAD_DIM
            out = jnp.zeros((rows2, LANES), F32)
            for which, v_x in enumerate(padded(b, vbuf, vsbuf, h // 2, h % 2)):
                sc = sbuf[b % 2, 2 * h + which]
                sink = jnp.where(top, sink_ref[h * GROUP + which], sink_ref[h * GROUP + 2 + which])
                m = jnp.maximum(jnp.max(sc, axis=-1, keepdims=True), sink)
                e = jnp.exp(sc - m)
                den = jnp.sum(e, axis=-1, keepdims=True) + jnp.exp(sink - m)
                p = e * (1.0 / den)
                pz = jnp.zeros_like(p)
                pcat = jnp.concatenate([jnp.where(upper, p, pz), jnp.where(upper, pz, p)], axis=1)
                out = out + _dot(pcat.astype(BF16), v_x)
            orows = slice(b * WINDOW, (b + 1) * WINDOW)
            abuf[orows, col:col + LANES] = out[:WINDOW].astype(BF16)
            abuf[orows, col + LANES:col + 2 * LANES] = out[WINDOW:].astype(BF16)

    def conv_branch_out(piece):
        cols = slice(piece * (D_MODEL // nblk), (piece + 1) * (D_MODEL // nblk))
        x1_ref[0, :, cols] = x_ref[0, :, cols] + _dot(c_ref[0], w_ref[Q_DIM:, cols])

    nblk = TQ_ATTN // WINDOW
    scores(0)
    for b in range(nblk):
        if b + 1 < nblk:
            scores(b + 1)
        conv_branch_out(b)
        softmax_pv(b)

    x1 = x1_ref[0] + _dot(abuf[...], w_ref[0:Q_DIM, :])
    x1_ref[0] = x1
    ms = jnp.mean(x1 * x1, axis=-1, keepdims=True)
    h_ref[0] = (x1 * lax.rsqrt(ms + EPS) * nw_ref[...]).astype(BF16)


def _attention_outproj(sinks, q, k, ks, v, vs, conv, x, w_out, nw):
    b, s, _ = q.shape
    nprev = TQ_ATTN // WINDOW
    cur = lambda n: pl.BlockSpec((1, TQ_ATTN, n), lambda bi, i: (bi, i, 0))
    prev = pl.BlockSpec((1, WINDOW, KV_DIM), lambda bi, i: (bi, jnp.maximum(i * nprev - 1, 0), 0))
    kv_scratch = pltpu.VMEM((TQ_ATTN + WINDOW, KV_DIM), BF16)
    return pl.pallas_call(
        _attn_kernel,
        grid=(b, s // TQ_ATTN),
        in_specs=[pl.BlockSpec(memory_space=pltpu.SMEM), cur(Q_DIM), cur(KV_DIM), cur(KV_DIM), cur(KV_DIM),
                  cur(KV_DIM), prev, prev, prev, prev, cur(CONV_CH), cur(D_MODEL),
                  _resident((D_MODEL, D_MODEL)), _resident((1, D_MODEL))],
        out_specs=[cur(D_MODEL), cur(D_MODEL)],
        out_shape=[jax.ShapeDtypeStruct((b, s, D_MODEL), F32), jax.ShapeDtypeStruct((b, s, D_MODEL), BF16)],
        scratch_shapes=[kv_scratch] * 4 + [pltpu.VMEM((2, 2 * N_KV_HEADS, 2 * WINDOW, WINDOW), F32),
                                           pltpu.VMEM((TQ_ATTN, Q_DIM), BF16)],
        compiler_params=pltpu.CompilerParams(dimension_semantics=("arbitrary", "arbitrary"),
                                             vmem_limit_bytes=VMEM_LIMIT),
        name="attention_outproj",
    )(sinks, q, k, ks, v, vs, k, ks, v, vs, conv, x, w_out, nw)


def _ffn_kernel(h_ref, x1_ref, wug_ref, wuv_ref, cwg_ref, cwv_ref, cbg_ref, cbv_ref, wd_ref, o_ref,
                gbuf, vbuf, carry, *, tiles_per_seq):
    i = pl.program_id(0)
    j = pl.program_id(1)
    hist = SUBLANES
    first_tile = i % tiles_per_seq == 0
    nsub = FC_FFN // MXU_DIM
    sub_cols = [slice(sub * MXU_DIM, (sub + 1) * MXU_DIM) for sub in range(nsub)]

    lane_cols = MXU_DIM // LANES

    def up(chunk):
        for sub, cols in enumerate(sub_cols):
            for kind, (w_ref, bufs) in enumerate(((wug_ref, gbuf), (wuv_ref, vbuf))):
                u = _dot(h_ref[...], w_ref[:, cols])
                for lc in range(lane_cols):
                    c = sub * lane_cols + lc
                    slot = (2 * c + kind) * N_FC + chunk
                    prev = carry[slot]
                    bufs[c, 0:hist, :] = jnp.where(first_tile, jnp.zeros_like(prev), prev)
                    bufs[c, hist:, :] = u[:, lc * LANES:(lc + 1) * LANES]
                    carry[slot] = bufs[c, TM_FFN:TM_FFN + hist, :]

    def conv3(bufs, sub, cw_ref, cb_ref, cols):
        ys = []
        for lc in range(lane_cols):
            c = sub * lane_cols + lc
            ls = slice(cols.start + lc * LANES, cols.start + (lc + 1) * LANES)
            y = cb_ref[:, ls]
            for t in range(FFN_CONV_K):
                lag = FFN_CONV_K - 1 - t
                y = y + cw_ref[t:t + 1, ls] * bufs[c, hist - lag:hist - lag + TM_FFN, :]
            ys.append(y)
        return jnp.concatenate(ys, axis=1)

    def activate_down():
        for sub, cols in enumerate(sub_cols):
            gate = conv3(gbuf, sub, cwg_ref, cbg_ref, cols)
            val = conv3(vbuf, sub, cwv_ref, cbv_ref, cols)
            act = (gate * jax.nn.sigmoid(gate) * val).astype(BF16)
            for nb in range(D_MODEL // DOWN_COLS):
                ncols = slice(nb * DOWN_COLS, (nb + 1) * DOWN_COLS)
                o_ref[:, ncols] += _dot(act, wd_ref[cols, ncols])

    @pl.when(j == 0)
    def _():
        o_ref[...] = jnp.zeros_like(o_ref)

    up(j)
    activate_down()

    @pl.when(j < X1_SLABS)
    def _():
        rows = pl.ds(pl.multiple_of(j * (TM_FFN // X1_SLABS), TM_FFN // X1_SLABS), TM_FFN // X1_SLABS)
        o_ref[rows, :] += x1_ref[...]


def _ffn(h2, x1, wu, cw, cb, wd, tiles_per_seq):
    m = h2.shape[0]
    slab = TM_FFN // X1_SLABS
    up_chunk = act_chunk = lambda j: j
    return pl.pallas_call(
        functools.partial(_ffn_kernel, tiles_per_seq=tiles_per_seq),
        grid=(m // TM_FFN, N_FC),
        in_specs=[pl.BlockSpec((TM_FFN, D_MODEL), lambda i, j: (i, 0)),
                  pl.BlockSpec((slab, D_MODEL), lambda i, j: (i * X1_SLABS + jnp.minimum(j, X1_SLABS - 1), 0)),
                  pl.BlockSpec((D_MODEL, FC_FFN), lambda i, j: (0, up_chunk(j))),
                  pl.BlockSpec((D_MODEL, FC_FFN), lambda i, j: (0, N_FC + up_chunk(j))),
                  pl.BlockSpec((FFN_CONV_K, FC_FFN), lambda i, j: (0, act_chunk(j))),
                  pl.BlockSpec((FFN_CONV_K, FC_FFN), lambda i, j: (0, N_FC + act_chunk(j))),
                  pl.BlockSpec((1, FC_FFN), lambda i, j: (0, act_chunk(j))),
                  pl.BlockSpec((1, FC_FFN), lambda i, j: (0, N_FC + act_chunk(j))),
                  pl.BlockSpec((FC_FFN, D_MODEL), lambda i, j: (act_chunk(j), 0))],
        out_specs=pl.BlockSpec((TM_FFN, D_MODEL), lambda i, j: (i, 0)),
        out_shape=jax.ShapeDtypeStruct((m, D_MODEL), F32),
        scratch_shapes=[pltpu.VMEM((FC_FFN // LANES, TM_FFN + SUBLANES, LANES), F32),
                        pltpu.VMEM((FC_FFN // LANES, TM_FFN + SUBLANES, LANES), F32),
                        pltpu.VMEM((2 * (FC_FFN // LANES) * N_FC, SUBLANES, LANES), F32)],
        compiler_params=pltpu.CompilerParams(dimension_semantics=("arbitrary", "arbitrary"),
                                             vmem_limit_bytes=VMEM_LIMIT),
        name="conv_ffn",
    )(h2, x1, wu, wu, cw, cw, cb, cb, wd)


def _layer(x, attn_norm_w, w_in, q_norm_w, k_norm_w, sinks, conv_dw_w, conv_dw_b, conv_ln_w, conv_ln_b,
           w_out, ffn_norm_w, w_up, ffn_dw_w, ffn_dw_b, w_down):
    b, s, d = x.shape
    m = b * s
    head =lax.broadcasted_iota(jnp.int32, (MXU_DIM, MXU_DIM), 0) // HEAD_DIM
    gmat = jnp.where(head == head.T, 1.0 / HEAD_DIM, 0.0).astype(BF16)
    qw = jnp.tile(q_norm_w, MXU_DIM // HEAD_DIM)[None, :]
    kw = jnp.tile(k_norm_w, KV_DIM // HEAD_DIM)[None, :]
    q, k, ks, v, vs, conv = _inproj(x, attn_norm_w[None, :], w_in.astype(BF16), gmat, qw, kw, conv_dw_w,
                                    conv_dw_b[None, :], conv_ln_w[None, :], conv_ln_b[None, :])
    x1, h2 = _attention_outproj(sinks, q, k, ks, v, vs, conv, x, w_out.astype(BF16), ffn_norm_w[None, :])
    out = _ffn(h2.reshape(m, d), x1.reshape(m, d), w_up.astype(BF16), ffn_dw_w, ffn_dw_b[None, :], w_down.astype(BF16), s // TM_FFN)
    return out.reshape(b, s, d)


def kernel(x, attn_norm_w, w_in, q_norm_w, k_norm_w, sinks, conv_dw_w, conv_dw_b, conv_ln_w, conv_ln_b, w_out,
           ffn_norm_w, w_up, ffn_dw_w, ffn_dw_b, w_down):
    for l in range(attn_norm_w.shape[0]):
        x = _layer(x, attn_norm_w[l], w_in[l], q_norm_w[l], k_norm_w[l], sinks[l], conv_dw_w[l], conv_dw_b[l],
                   conv_ln_w[l], conv_ln_b[l], w_out[l], ffn_norm_w[l], w_up[l], ffn_dw_w[l], ffn_dw_b[l],
                   w_down[l])
    return x
```

```python
import functools

import jax
import jax.numpy as jnp
from jax import lax
from jax.experimental import pallas as pl
from jax.experimental.pallas import tpu as pltpu

F32 = jnp.float32
BF16 = jnp.bfloat16

D_MODEL = 2048
N_Q_HEADS = 16
N_KV_HEADS = 4
HEAD_DIM = 64
GROUP = N_Q_HEADS // N_KV_HEADS
Q_DIM = N_Q_HEADS * HEAD_DIM
KV_DIM = N_KV_HEADS * HEAD_DIM
WINDOW = 128
CONV_CH = D_MODEL - Q_DIM
CONV_K = 31
D_FF = 5632
FFN_CONV_K = 3
EPS = 1e-6

LANES = 128
SUBLANES = 8
MXU_DIM = 256
VMEM_LIMIT = 62 * 1024 * 1024

TQ_ATTN = 512
TS_CONV = 512
CONV_HALO = 32
CONV_ROWS = 64
TM_FFN = 1024
X1_SLABS = 8
FC_FFN = 512
N_FC = D_FF // FC_FFN
DOWN_COLS = 512


def _dot(a, b):
    return jnp.dot(a, b, preferred_element_type=F32)


def _dot_nt(a, b):
    return lax.dot_general(a, b, (((1,), (1,)), ((), ())), preferred_element_type=F32)


def _zero_after(t):
    def z(v):
        bits = pltpu.bitcast(v, jnp.uint32)
        return ((bits >> 16) >> 16).astype(F32)
    return z(t[0:SUBLANES, 0:LANES])[0:1] + z(t[-SUBLANES:, -LANES:])[0:1]


def _resident(shape):
    return pl.BlockSpec(shape, lambda *_: (0,) * len(shape), pipeline_mode=pl.Buffered(1))


def _inproj_kernel(x_ref, nw_ref, w_ref, g_ref, qw_ref, kw_ref, cw_ref, cb_ref, lw_ref, lb_ref,
                   q_ref, kv_ref, c_ref, cbuf, ybuf):
    i = pl.program_id(1)
    x = x_ref[0]
    ms = jnp.mean(x * x, axis=-1, keepdims=True)
    h = (x * lax.rsqrt(ms + EPS) * nw_ref[...]).astype(BF16)
    g = g_ref[...]

    def head_norm(t, w):
        sq = t * t
        hi = sq.astype(BF16)
        lo = (sq - hi.astype(F32)).astype(BF16)
        hms = _dot(hi, g) + _dot(lo, g)
        return t * lax.rsqrt(hms + EPS) * w

    def swap_halves(t):
        return jnp.concatenate(
            [pltpu.roll(t[:, i * LANES:(i + 1) * LANES], HEAD_DIM, axis=1)
             for i in range(t.shape[1] // LANES)], axis=1)

    a0 = Q_DIM + 2 * KV_DIM
    g0 = a0 + CONV_CH
    nparts = CONV_CH // MXU_DIM
    ncol = CONV_CH // LANES

    @pl.when(i == 0)
    def _():
        cbuf[:, 0:CONV_HALO, :] = jnp.zeros((ncol, CONV_HALO, LANES), F32)

    def glu(part):
        a = _dot(h, w_ref[:, a0 + part * MXU_DIM:a0 + (part + 1) * MXU_DIM])
        gt = _dot(h, w_ref[:, g0 + part * MXU_DIM:g0 + (part + 1) * MXU_DIM])
        c = a * jax.nn.sigmoid(gt)
        for lc in range(MXU_DIM // LANES):
            cbuf[part * (MXU_DIM // LANES) + lc, CONV_HALO:, :] = c[:, lc * LANES:(lc + 1) * LANES]

    off = CONV_HALO - (CONV_K - 1)

    def conv_column(cb, after=None):
        ls = slice(cb * LANES, (cb + 1) * LANES)
        bias = cb_ref[:, ls] if after is None else cb_ref[:, ls] + _zero_after(after)
        for rb in range(TS_CONV // CONV_ROWS):
            r0 = rb * CONV_ROWS
            acc = jnp.broadcast_to(bias, (CONV_ROWS, LANES))
            for t in range(CONV_K):
                acc = acc + cw_ref[t:t + 1, ls] * cbuf[cb, r0 + off + t:r0 + off + t + CONV_ROWS, :]
            ybuf[r0:r0 + CONV_ROWS, ls] = acc

    for part in range(nparts):
        glu(part)

    scale = 1.0 / (HEAD_DIM ** 0.5)
    wide = 2 * MXU_DIM
    proj = [_dot(h, w_ref[:, n * wide:(n + 1) * wide]) for n in range((Q_DIM + 2 * KV_DIM) // wide)]
    for n in range(Q_DIM // wide):
        for j in range(wide // MXU_DIM):
            sl = slice(j * MXU_DIM, (j + 1) * MXU_DIM)
            qn = head_norm(proj[n][:, sl], qw_ref[...]) * scale
            q_ref[0, :, n * wide + j * MXU_DIM:n * wide + (j + 1) * MXU_DIM] = qn.astype(BF16)
    kv = proj[Q_DIM // wide]
    k = head_norm(kv[:, 0:KV_DIM], kw_ref[...])
    kv_ref[0, :, 0 * KV_DIM:1 * KV_DIM] = k.astype(BF16)
    kv_ref[0, :, 1 * KV_DIM:2 * KV_DIM] = swap_halves(k).astype(BF16)
    v = kv[:, KV_DIM:2 * KV_DIM]
    kv_ref[0, :, 2 * KV_DIM:3 * KV_DIM] = v.astype(BF16)
    kv_ref[0, :, 3 * KV_DIM:4 * KV_DIM] = swap_halves(v).astype(BF16)

    for cb in range(ncol):
        n = cb - (ncol - 2)
        conv_column(cb, proj[n] if n >= 0 else None)

    for rb in range(TS_CONV // CONV_ROWS):
        rows = slice(rb * CONV_ROWS, (rb + 1) * CONV_ROWS)
        y = ybuf[rows, :]
        mu = jnp.mean(y, axis=-1, keepdims=True)
        d = y - mu
        var = jnp.mean(d * d, axis=-1, keepdims=True)
        z = d * lax.rsqrt(var + EPS) * lw_ref[...] + lb_ref[...]
        c_ref[0, rows, :] = (z * jax.nn.sigmoid(z)).astype(BF16)

    cbuf[:, 0:CONV_HALO, :] = cbuf[:, TS_CONV:TS_CONV + CONV_HALO, :]


def _inproj(x, nw, w_in, gmat, qw, kw, cw, cb, lw, lb):
    b, s, _ = x.shape
    in_cols = w_in.shape[1]
    row = lambda n: pl.BlockSpec((1, TS_CONV, n), lambda bi, i: (bi, i, 0))
    return pl.pallas_call(
        _inproj_kernel,
        grid=(b, s // TS_CONV),
        in_specs=[row(D_MODEL), _resident((1, D_MODEL)), _resident((D_MODEL, in_cols)),
                  _resident((MXU_DIM, MXU_DIM)), _resident((1, MXU_DIM)), _resident((1, KV_DIM)),
                  _resident((CONV_K, CONV_CH)), _resident((1, CONV_CH)), _resident((1, CONV_CH)),
                  _resident((1, CONV_CH))],
        out_specs=[row(Q_DIM), row(4 * KV_DIM), row(CONV_CH)],
        out_shape=[jax.ShapeDtypeStruct((b, s, Q_DIM), BF16), jax.ShapeDtypeStruct((b, s, 4 * KV_DIM), BF16),
                   jax.ShapeDtypeStruct((b, s, CONV_CH), BF16)],
        scratch_shapes=[pltpu.VMEM((CONV_CH // LANES, TS_CONV + CONV_HALO, LANES), F32),
                        pltpu.VMEM((TS_CONV, CONV_CH), F32)],
        compiler_params=pltpu.CompilerParams(dimension_semantics=("arbitrary", "arbitrary"),
                                             vmem_limit_bytes=VMEM_LIMIT),
        name="inproj_conv",
    )(x, nw, w_in, gmat, qw, kw, cw, cb, lw, lb)


def _attn_kernel(sink_ref, q_ref, kv_ref, kvp_ref, c_ref, x_ref, w_ref, nw_ref, x1_ref, h_ref,
                 kvbuf, sbuf, abuf):
    i = pl.program_id(1)
    kvbuf[0:WINDOW, :] = kvp_ref[0]
    kvbuf[WINDOW:, :] = kv_ref[0]
    k_own, k_swapped, v_own, v_swapped = range(4)

    rows2 = 2 * WINDOW
    r = lax.broadcasted_iota(jnp.int32, (rows2, WINDOW), 0) % WINDOW
    c = lax.broadcasted_iota(jnp.int32, (rows2, WINDOW), 1)
    upper = c > r
    top = lax.broadcasted_iota(jnp.int32, (rows2, 1), 0) < WINDOW
    lo = lax.broadcasted_iota(jnp.int32, (rows2, LANES), 1) < HEAD_DIM
    first_bias = jnp.where(i == 0, -jnp.inf, 0.0).astype(F32)

    def padded(b, own_grp, swapped_grp, hp, half):
        band = slice(b * WINDOW, (b + 2) * WINDOW)
        own, oth = (own_grp, swapped_grp) if half == 0 else (swapped_grp, own_grp)
        lanes = lambda grp: slice(grp * KV_DIM + hp * LANES, grp * KV_DIM + (hp + 1) * LANES)
        zero = jnp.zeros((rows2, LANES), BF16)
        return jnp.where(lo, kvbuf[band, lanes(own)], zero), jnp.where(lo, zero, kvbuf[band, lanes(oth)])

    def scores(b):
        for h in range(N_KV_HEADS):
            col = h * GROUP * HEAD_DIM
            qrows = slice(b * WINDOW, (b + 1) * WINDOW)
            q2 = jnp.concatenate([q_ref[0, qrows, col:col + LANES], q_ref[0, qrows, col + LANES:col + 2 * LANES]],
                                 axis=0)
            for which, k_x in enumerate(padded(b, k_own, k_swapped, h // 2, h % 2)):
                s = _dot_nt(q2, k_x)
                s_prev = s[:, :WINDOW] + first_bias if b == 0 else s[:, :WINDOW]
                sbuf[b % 2, 2 * h + which] = jnp.where(upper, s_prev, s[:, WINDOW:])

    def softmax_pv(b):
        for h in range(N_KV_HEADS):
            col = h * GROUP * HEAD_DIM
            out = jnp.zeros((rows2, LANES), F32)
            for which, v_x in enumerate(padded(b, v_own, v_swapped, h // 2, h % 2)):
                sc = sbuf[b % 2, 2 * h + which]
                sink = jnp.where(top, sink_ref[h * GROUP + which], sink_ref[h * GROUP + 2 + which])
                m = jnp.maximum(jnp.max(sc, axis=-1, keepdims=True), sink)
                e = jnp.exp(sc - m)
                den = jnp.sum(e, axis=-1, keepdims=True) + jnp.exp(sink - m)
                p = e * (1.0 / den)
                pz = jnp.zeros_like(p)
                pcat = jnp.concatenate([jnp.where(upper, p, pz), jnp.where(upper, pz, p)], axis=1)
                out = out + _dot(pcat.astype(BF16), v_x)
            orows = slice(b * WINDOW, (b + 1) * WINDOW)
            abuf[orows, col:col + LANES] = out[:WINDOW].astype(BF16)
            abuf[orows, col + LANES:col + 2 * LANES] = out[WINDOW:].astype(BF16)

    def conv_branch_out(piece):
        cols = slice(piece * (D_MODEL // nblk), (piece + 1) * (D_MODEL // nblk))
        x1_ref[0, :, cols] = x_ref[0, :, cols] + _dot(c_ref[0], w_ref[Q_DIM:, cols])

    nblk = TQ_ATTN // WINDOW
    scores(0)
    for b in range(nblk):
        if b + 1 < nblk:
            scores(b + 1)
        conv_branch_out(b)
        softmax_pv(b)

    x1 = x1_ref[0] + _dot(abuf[...], w_ref[0:Q_DIM, :])
    x1_ref[0] = x1
    ms = jnp.mean(x1 * x1, axis=-1, keepdims=True)
    h_ref[0] = (x1 * lax.rsqrt(ms + EPS) * nw_ref[...]).astype(BF16)


def _attention_outproj(sinks, q, kv, conv, x, w_out, nw):
    b, s, _ = q.shape
    nprev = TQ_ATTN // WINDOW
    cur = lambda n: pl.BlockSpec((1, TQ_ATTN, n), lambda bi, i: (bi, i, 0))
    prev = pl.BlockSpec((1, WINDOW, 4 * KV_DIM), lambda bi, i: (bi, jnp.maximum(i * nprev - 1, 0), 0))
    return pl.pallas_call(
        _attn_kernel,
        grid=(b, s // TQ_ATTN),
        in_specs=[pl.BlockSpec(memory_space=pltpu.SMEM), cur(Q_DIM), cur(4 * KV_DIM), prev, cur(CONV_CH),
                  cur(D_MODEL), _resident((D_MODEL, D_MODEL)), _resident((1, D_MODEL))],
        out_specs=[cur(D_MODEL), cur(D_MODEL)],
        out_shape=[jax.ShapeDtypeStruct((b, s, D_MODEL), F32), jax.ShapeDtypeStruct((b, s, D_MODEL), BF16)],
        scratch_shapes=[pltpu.VMEM((TQ_ATTN + WINDOW, 4 * KV_DIM), BF16),
                        pltpu.VMEM((2, 2 * N_KV_HEADS, 2 * WINDOW, WINDOW), F32),
                        pltpu.VMEM((TQ_ATTN, Q_DIM), BF16)],
        compiler_params=pltpu.CompilerParams(dimension_semantics=("arbitrary", "arbitrary"),
                                             vmem_limit_bytes=VMEM_LIMIT),
        name="attention_outproj",
    )(sinks, q, kv, kv, conv, x, w_out, nw)


def _ffn_kernel(h_ref, x1_ref, wug_ref, wuv_ref, cw_ref, cb_ref, wd_ref, o_ref,
                gbuf, vbuf, carry, *, tiles_per_seq):
    i = pl.program_id(0)
    j = pl.program_id(1)
    hist = SUBLANES
    first_tile = i % tiles_per_seq == 0
    nsub = FC_FFN // MXU_DIM
    sub_cols = [slice(sub * MXU_DIM, (sub + 1) * MXU_DIM) for sub in range(nsub)]

    lane_cols = MXU_DIM // LANES

    def up(chunk):
        for sub, cols in enumerate(sub_cols):
            for kind, (w_ref, bufs) in enumerate(((wug_ref, gbuf), (wuv_ref, vbuf))):
                u = _dot(h_ref[...], w_ref[:, cols])
                for lc in range(lane_cols):
                    c = sub * lane_cols + lc
                    slot = (2 * c + kind) * N_FC + chunk
                    prev = carry[slot]
                    bufs[c, 0:hist, :] = jnp.where(first_tile, jnp.zeros_like(prev), prev)
                    bufs[c, hist:, :] = u[:, lc * LANES:(lc + 1) * LANES]
                    carry[slot] = bufs[c, TM_FFN:TM_FFN + hist, :]

    def conv3(bufs, sub, kind, cols):
        row = kind * N_FC + j
        ys = []
        for lc in range(lane_cols):
            c = sub * lane_cols + lc
            ls = slice(cols.start + lc * LANES, cols.start + (lc + 1) * LANES)
            y = cb_ref[row, :, ls]
            for t in range(FFN_CONV_K):
                lag = FFN_CONV_K - 1 - t
                y = y + cw_ref[row, t:t + 1, ls] * bufs[c, hist - lag:hist - lag + TM_FFN, :]
            ys.append(y)
        return jnp.concatenate(ys, axis=1)

    def activate_down():
        for sub, cols in enumerate(sub_cols):
            gate = conv3(gbuf, sub, 0, cols)
            val = conv3(vbuf, sub, 1, cols)
            act = (gate * jax.nn.sigmoid(gate) * val).astype(BF16)
            for nb in range(D_MODEL // DOWN_COLS):
                ncols = slice(nb * DOWN_COLS, (nb + 1) * DOWN_COLS)
                o_ref[:, ncols] += _dot(act, wd_ref[cols, ncols])

    @pl.when(j == 0)
    def _():
        o_ref[...] = jnp.zeros_like(o_ref)

    up(j)
    activate_down()

    @pl.when(j < X1_SLABS)
    def _():
        rows = pl.ds(pl.multiple_of(j * (TM_FFN // X1_SLABS), TM_FFN // X1_SLABS), TM_FFN // X1_SLABS)
        o_ref[rows, :] += x1_ref[...]


def _ffn(h2, x1, wu, cw, cb, wd, tiles_per_seq):
    m = h2.shape[0]
    slab = TM_FFN // X1_SLABS
    cw3 = cw.reshape(FFN_CONV_K, 2 * N_FC, FC_FFN).transpose(1, 0, 2)
    cb3 = cb.reshape(2 * N_FC, 1, FC_FFN)
    return pl.pallas_call(
        functools.partial(_ffn_kernel, tiles_per_seq=tiles_per_seq),
        grid=(m // TM_FFN, N_FC),
        in_specs=[pl.BlockSpec((TM_FFN, D_MODEL), lambda i, j: (i, 0)),
                  pl.BlockSpec((slab, D_MODEL), lambda i, j: (i * X1_SLABS + jnp.minimum(j, X1_SLABS - 1), 0)),
                  pl.BlockSpec((D_MODEL, FC_FFN), lambda i, j: (0, j)),
                  pl.BlockSpec((D_MODEL, FC_FFN), lambda i, j: (0, N_FC + j)),
                  _resident((2 * N_FC, FFN_CONV_K, FC_FFN)), _resident((2 * N_FC, 1, FC_FFN)),
                  pl.BlockSpec((FC_FFN, D_MODEL), lambda i, j: (j, 0))],
        out_specs=pl.BlockSpec((TM_FFN, D_MODEL), lambda i, j: (i, 0)),
        out_shape=jax.ShapeDtypeStruct((m, D_MODEL), F32),
        scratch_shapes=[pltpu.VMEM((FC_FFN // LANES, TM_FFN + SUBLANES, LANES), F32),
                        pltpu.VMEM((FC_FFN // LANES, TM_FFN + SUBLANES, LANES), F32),
                        pltpu.VMEM((2 * (FC_FFN // LANES) * N_FC, SUBLANES, LANES), F32)],
        compiler_params=pltpu.CompilerParams(dimension_semantics=("arbitrary", "arbitrary"),
                                             vmem_limit_bytes=VMEM_LIMIT),
        name="conv_ffn",
    )(h2, x1, wu, wu, cw3, cb3, wd)


def _layer(x, attn_norm_w, w_in, q_norm_w, k_norm_w, sinks, conv_dw_w, conv_dw_b, conv_ln_w, conv_ln_b,
           w_out, ffn_norm_w, w_up, ffn_dw_w, ffn_dw_b, w_down):
    b, s, d = x.shape
    m = b * s
    head =lax.broadcasted_iota(jnp.int32, (MXU_DIM, MXU_DIM), 0) // HEAD_DIM
    gmat = jnp.where(head == head.T, 1.0 / HEAD_DIM, 0.0).astype(BF16)
    qw = jnp.tile(q_norm_w, MXU_DIM // HEAD_DIM)[None, :]
    kw = jnp.tile(k_norm_w, KV_DIM // HEAD_DIM)[None, :]
    q, kv, conv = _inproj(x, attn_norm_w[None, :], w_in.astype(BF16), gmat, qw, kw, conv_dw_w,
                          conv_dw_b[None, :], conv_ln_w[None, :], conv_ln_b[None, :])
    x1, h2 = _attention_outproj(sinks, q, kv, conv, x, w_out.astype(BF16), ffn_norm_w[None, :])
    out = _ffn(h2.reshape(m, d), x1.reshape(m, d), w_up.astype(BF16), ffn_dw_w, ffn_dw_b, w_down.astype(BF16),
               s // TM_FFN)
    return out.reshape(b, s, d)


def kernel(x, attn_norm_w, w_in, q_norm_w, k_norm_w, sinks, conv_dw_w, conv_dw_b, conv_ln_w, conv_ln_b, w_out,
           ffn_norm_w, w_up, ffn_dw_w, ffn_dw_b, w_down):
    for l in range(attn_norm_w.shape[0]):
        x = _layer(x, attn_norm_w[l], w_in[l], q_norm_w[l], k_norm_w[l], sinks[l], conv_dw_w[l], conv_dw_b[l],
                   conv_ln_w[l], conv_ln_b[l], w_out[l], ffn_norm_w[l], w_up[l], ffn_dw_w[l], ffn_dw_b[l],
                   w_down[l])
    return x
```

```python
import functools

import jax
import jax.numpy as jnp
from jax import lax
from jax.experimental import pallas as pl
from jax.experimental.pallas import tpu as pltpu

F32 = jnp.float32
BF16 = jnp.bfloat16

D_MODEL = 2048
N_Q_HEADS = 16
N_KV_HEADS = 4
HEAD_DIM = 64
GROUP = N_Q_HEADS // N_KV_HEADS
Q_DIM = N_Q_HEADS * HEAD_DIM
KV_DIM = N_KV_HEADS * HEAD_DIM
WINDOW = 128
CONV_CH = D_MODEL - Q_DIM
CONV_K = 31
D_FF = 5632
FFN_CONV_K = 3
EPS = 1e-6

LANES = 128
SUBLANES = 8
MXU_DIM = 256
VMEM_LIMIT = 62 * 1024 * 1024

TQ_ATTN = 512
TS_CONV = 512
CONV_HALO = 32
CONV_ROWS = 64
TM_FFN = 1024
X1_SLABS = 8
FC_FFN = 512
N_FC = D_FF // FC_FFN
DOWN_COLS = 512


def _dot(a, b):
    return jnp.dot(a, b, preferred_element_type=F32)


def _dot_nt(a, b):
    return lax.dot_general(a, b, (((1,), (1,)), ((), ())), preferred_element_type=F32)


def _zero_after(t):
    def z(v):
        bits = pltpu.bitcast(v, jnp.uint32)
        return ((bits >> 16) >> 16).astype(F32)
    return z(t[0:SUBLANES, 0:LANES])[0:1] + z(t[-SUBLANES:, -LANES:])[0:1]


def _resident(shape):
    return pl.BlockSpec(shape, lambda *_: (0,) * len(shape), pipeline_mode=pl.Buffered(1))


def _inproj_kernel(x_ref, nw_ref, w_ref, g_ref, qw_ref, kw_ref, cw_ref, cb_ref, lw_ref, lb_ref,
                   q_ref, kv_ref, c_ref, cbuf, ybuf):
    i = pl.program_id(1)
    x = x_ref[0]
    ms = jnp.mean(x * x, axis=-1, keepdims=True)
    h = (x * lax.rsqrt(ms + EPS) * nw_ref[...]).astype(BF16)
    g = g_ref[...]

    def head_norm(t, w):
        sq = t * t
        hi = sq.astype(BF16)
        lo = (sq - hi.astype(F32)).astype(BF16)
        hms = _dot(hi, g) + _dot(lo, g)
        return t * lax.rsqrt(hms + EPS) * w

    def swap_halves(t):
        return jnp.concatenate(
            [pltpu.roll(t[:, i * LANES:(i + 1) * LANES], HEAD_DIM, axis=1)
             for i in range(t.shape[1] // LANES)], axis=1)

    a0 = Q_DIM + 2 * KV_DIM
    g0 = a0 + CONV_CH
    nparts = CONV_CH // MXU_DIM
    ncol = CONV_CH // LANES

    @pl.when(i == 0)
    def _():
        cbuf[:, 0:CONV_HALO, :] = jnp.zeros((ncol, CONV_HALO, LANES), F32)

    def glu(part):
        a = _dot(h, w_ref[:, a0 + part * MXU_DIM:a0 + (part + 1) * MXU_DIM])
        gt = _dot(h, w_ref[:, g0 + part * MXU_DIM:g0 + (part + 1) * MXU_DIM])
        c = a * jax.nn.sigmoid(gt)
        for lc in range(MXU_DIM // LANES):
            cbuf[part * (MXU_DIM // LANES) + lc, CONV_HALO:, :] = c[:, lc * LANES:(lc + 1) * LANES]

    off = CONV_HALO - (CONV_K - 1)

    def conv_column(cb, after=None):
        ls = slice(cb * LANES, (cb + 1) * LANES)
        bias = cb_ref[:, ls] if after is None else cb_ref[:, ls] + _zero_after(after)
        for rb in range(TS_CONV // CONV_ROWS):
            r0 = rb * CONV_ROWS
            acc = jnp.broadcast_to(bias, (CONV_ROWS, LANES))
            for t in range(CONV_K):
                acc = acc + cw_ref[t:t + 1, ls] * cbuf[cb, r0 + off + t:r0 + off + t + CONV_ROWS, :]
            ybuf[r0:r0 + CONV_ROWS, ls] = acc

    for part in range(nparts):
        glu(part)

    scale = 1.0 / (HEAD_DIM ** 0.5)
    wide = 2 * MXU_DIM
    proj = [_dot(h, w_ref[:, n * wide:(n + 1) * wide]) for n in range((Q_DIM + 2 * KV_DIM) // wide)]
    for n in range(Q_DIM // wide):
        for j in range(wide // MXU_DIM):
            sl = slice(j * MXU_DIM, (j + 1) * MXU_DIM)
            qn = head_norm(proj[n][:, sl], qw_ref[...]) * scale
            q_ref[0, :, n * wide + j * MXU_DIM:n * wide + (j + 1) * MXU_DIM] = qn.astype(BF16)
    kv = proj[Q_DIM // wide]
    k = head_norm(kv[:, 0:KV_DIM], kw_ref[...])
    kv_ref[0, :, 0 * KV_DIM:1 * KV_DIM] = k.astype(BF16)
    kv_ref[0, :, 1 * KV_DIM:2 * KV_DIM] = swap_halves(k).astype(BF16)
    v = kv[:, KV_DIM:2 * KV_DIM]
    kv_ref[0, :, 2 * KV_DIM:3 * KV_DIM] = v.astype(BF16)
    kv_ref[0, :, 3 * KV_DIM:4 * KV_DIM] = swap_halves(v).astype(BF16)

    for cb in range(ncol):
        n = cb - (ncol - 2)
        conv_column(cb, proj[n] if n >= 0 else None)

    for rb in range(TS_CONV // CONV_ROWS):
        rows = slice(rb * CONV_ROWS, (rb + 1) * CONV_ROWS)
        y = ybuf[rows, :]
        mu = jnp.mean(y, axis=-1, keepdims=True)
        d = y - mu
        var = jnp.mean(d * d, axis=-1, keepdims=True)
        z = d * lax.rsqrt(var + EPS) * lw_ref[...] + lb_ref[...]
        c_ref[0, rows, :] = (z * jax.nn.sigmoid(z)).astype(BF16)

    cbuf[:, 0:CONV_HALO, :] = cbuf[:, TS_CONV:TS_CONV + CONV_HALO, :]


def _inproj(x, nw, w_in, gmat, qw, kw, cw, cb, lw, lb):
    b, s, _ = x.shape
    in_cols = w_in.shape[1]
    row = lambda n: pl.BlockSpec((1, TS_CONV, n), lambda bi, i: (bi, i, 0))
    return pl.pallas_call(
        _inproj_kernel,
        grid=(b, s // TS_CONV),
        in_specs=[row(D_MODEL), _resident((1, D_MODEL)), _resident((D_MODEL, in_cols)),
                  _resident((MXU_DIM, MXU_DIM)), _resident((1, MXU_DIM)), _resident((1, KV_DIM)),
                  _resident((CONV_K, CONV_CH)), _resident((1, CONV_CH)), _resident((1, CONV_CH)),
                  _resident((1, CONV_CH))],
        out_specs=[row(Q_DIM), row(4 * KV_DIM), row(CONV_CH)],
        out_shape=[jax.ShapeDtypeStruct((b, s, Q_DIM), BF16), jax.ShapeDtypeStruct((b, s, 4 * KV_DIM), BF16),
                   jax.ShapeDtypeStruct((b, s, CONV_CH), BF16)],
        scratch_shapes=[pltpu.VMEM((CONV_CH // LANES, TS_CONV + CONV_HALO, LANES), F32),
                        pltpu.VMEM((TS_CONV, CONV_CH), F32)],
        compiler_params=pltpu.CompilerParams(dimension_semantics=("arbitrary", "arbitrary"),
                                             vmem_limit_bytes=VMEM_LIMIT),
        name="inproj_conv",
    )(x, nw, w_in, gmat, qw, kw, cw, cb, lw, lb)


def _attn_kernel(sink_ref, q_ref, kv_ref, kvp_ref, c_ref, x_ref, w_ref, nw_ref, x1_ref, h_ref,
                 kvbuf, sbuf, abuf):
    i = pl.program_id(1)
    kvbuf[0:WINDOW, :] = kvp_ref[0]
    kvbuf[WINDOW:, :] = kv_ref[0]
    k_own, k_swapped, v_own, v_swapped = range(4)

    rows2 = 2 * WINDOW
    r = lax.broadcasted_iota(jnp.int32, (rows2, WINDOW), 0) % WINDOW
    c = lax.broadcasted_iota(jnp.int32, (rows2, WINDOW), 1)
    upper = c > r
    top = lax.broadcasted_iota(jnp.int32, (rows2, 1), 0) < WINDOW
    lo = lax.broadcasted_iota(jnp.int32, (rows2, LANES), 1) < HEAD_DIM
    first_bias = jnp.where(i == 0, -jnp.inf, 0.0).astype(F32)

    def padded(b, own_grp, swapped_grp, hp, half):
        band = slice(b * WINDOW, (b + 2) * WINDOW)
        own, oth = (own_grp, swapped_grp) if half == 0 else (swapped_grp, own_grp)
        lanes = lambda grp: slice(grp * KV_DIM + hp * LANES, grp * KV_DIM + (hp + 1) * LANES)
        zero = jnp.zeros((rows2, LANES), BF16)
        return jnp.where(lo, kvbuf[band, lanes(own)], zero), jnp.where(lo, zero, kvbuf[band, lanes(oth)])

    def scores(b):
        for h in range(N_KV_HEADS):
            col = h * GROUP * HEAD_DIM
            qrows = slice(b * WINDOW, (b + 1) * WINDOW)
            q2 = jnp.concatenate([q_ref[0, qrows, col:col + LANES], q_ref[0, qrows, col + LANES:col + 2 * LANES]],
                                 axis=0)
            for which, k_x in enumerate(padded(b, k_own, k_swapped, h // 2, h % 2)):
                s = _dot_nt(q2, k_x)
                s_prev = s[:, :WINDOW] + first_bias if b == 0 else s[:, :WINDOW]
                sbuf[b % 2, 2 * h + which] = jnp.where(upper, s_prev, s[:, WINDOW:])

    def softmax_pv(b):
        for h in range(N_KV_HEADS):
            col = h * GROUP * HEAD_DIM
            out = jnp.zeros((rows2, LANES), F32)
            for which, v_x in enumerate(padded(b, v_own, v_swapped, h // 2, h % 2)):
                sc = sbuf[b % 2, 2 * h + which]
                sink = jnp.where(top, sink_ref[h * GROUP + which], sink_ref[h * GROUP + 2 + which])
                m = jnp.maximum(jnp.max(sc, axis=-1, keepdims=True), sink)
                e = jnp.exp(sc - m)
                den = jnp.sum(e, axis=-1, keepdims=True) + jnp.exp(sink - m)
                p = e * (1.0 / den)
                pz = jnp.zeros_like(p)
                pcat = jnp.concatenate([jnp.where(upper, p, pz), jnp.where(upper, pz, p)], axis=1)
                out = out + _dot(pcat.astype(BF16), v_x)
            orows = slice(b * WINDOW, (b + 1) * WINDOW)
            abuf[orows, col:col + LANES] = out[:WINDOW].astype(BF16)
            abuf[orows, col + LANES:col + 2 * LANES] = out[WINDOW:].astype(BF16)

    def conv_branch_out(piece):
        cols = slice(piece * (D_MODEL // nblk), (piece + 1) * (D_MODEL // nblk))
        x1_ref[0, :, cols] = x_ref[0, :, cols] + _dot(c_ref[0], w_ref[Q_DIM:, cols])

    nblk = TQ_ATTN // WINDOW
    scores(0)
    for b in range(nblk):
        if b + 1 < nblk:
            scores(b + 1)
        conv_branch_out(b)
        softmax_pv(b)

    x1 = x1_ref[0] + _dot(abuf[...], w_ref[0:Q_DIM, :])
    x1_ref[0] = x1
    ms = jnp.mean(x1 * x1, axis=-1, keepdims=True)
    h_ref[0] = (x1 * lax.rsqrt(ms + EPS) * nw_ref[...]).astype(BF16)


def _attention_outproj(sinks, q, kv, conv, x, w_out, nw):
    b, s, _ = q.shape
    nprev = TQ_ATTN // WINDOW
    cur = lambda n: pl.BlockSpec((1, TQ_ATTN, n), lambda bi, i: (bi, i, 0))
    prev = pl.BlockSpec((1, WINDOW, 4 * KV_DIM), lambda bi, i: (bi, jnp.maximum(i * nprev - 1, 0), 0))
    return pl.pallas_call(
        _attn_kernel,
        grid=(b, s // TQ_ATTN),
        in_specs=[pl.BlockSpec(memory_space=pltpu.SMEM), cur(Q_DIM), cur(4 * KV_DIM), prev, cur(CONV_CH),
                  cur(D_MODEL), _resident((D_MODEL, D_MODEL)), _resident((1, D_MODEL))],
        out_specs=[cur(D_MODEL), cur(D_MODEL)],
        out_shape=[jax.ShapeDtypeStruct((b, s, D_MODEL), F32), jax.ShapeDtypeStruct((b, s, D_MODEL), BF16)],
        scratch_shapes=[pltpu.VMEM((TQ_ATTN + WINDOW, 4 * KV_DIM), BF16),
                        pltpu.VMEM((2, 2 * N_KV_HEADS, 2 * WINDOW, WINDOW), F32),
                        pltpu.VMEM((TQ_ATTN, Q_DIM), BF16)],
        compiler_params=pltpu.CompilerParams(dimension_semantics=("arbitrary", "arbitrary"),
                                             vmem_limit_bytes=VMEM_LIMIT),
        name="attention_outproj",
    )(sinks, q, kv, kv, conv, x, w_out, nw)


def _ffn_kernel(h_ref, x1_ref, wug_ref, wuv_ref, cw_ref, cb_ref, wd_ref, o_ref,
                gbuf, vbuf, carry, *, tiles_per_seq):
    i = pl.program_id(0)
    j = pl.program_id(1)
    hist = SUBLANES
    first_tile = i % tiles_per_seq == 0
    nsub = FC_FFN // MXU_DIM
    sub_cols = [slice(sub * MXU_DIM, (sub + 1) * MXU_DIM) for sub in range(nsub)]

    lane_cols = MXU_DIM // LANES

    def up(chunk):
        for sub, cols in enumerate(sub_cols):
            for kind, (w_ref, bufs) in enumerate(((wug_ref, gbuf), (wuv_ref, vbuf))):
                u = _dot(h_ref[...], w_ref[:, cols].astype(BF16))
                for lc in range(lane_cols):
                    c = sub * lane_cols + lc
                    slot = (2 * c + kind) * N_FC + chunk
                    prev = carry[slot]
                    bufs[c, 0:hist, :] = jnp.where(first_tile, jnp.zeros_like(prev), prev)
                    bufs[c, hist:, :] = u[:, lc * LANES:(lc + 1) * LANES]
                    carry[slot] = bufs[c, TM_FFN:TM_FFN + hist, :]

    def conv3(bufs, sub, kind, cols):
        row = kind * N_FC + j
        ys = []
        for lc in range(lane_cols):
            c = sub * lane_cols + lc
            ls = slice(cols.start + lc * LANES, cols.start + (lc + 1) * LANES)
            y = cb_ref[row, :, ls]
            for t in range(FFN_CONV_K):
                lag = FFN_CONV_K - 1 - t
                y = y + cw_ref[row, t:t + 1, ls] * bufs[c, hist - lag:hist - lag + TM_FFN, :]
            ys.append(y)
        return jnp.concatenate(ys, axis=1)

    def activate_down():
        for sub, cols in enumerate(sub_cols):
            gate = conv3(gbuf, sub, 0, cols)
            val = conv3(vbuf, sub, 1, cols)
            act = (gate * jax.nn.sigmoid(gate) * val).astype(BF16)
            for nb in range(D_MODEL // DOWN_COLS):
                ncols = slice(nb * DOWN_COLS, (nb + 1) * DOWN_COLS)
                o_ref[:, ncols] += _dot(act, wd_ref[cols, ncols].astype(BF16))

    @pl.when(j == 0)
    def _():
        o_ref[...] = jnp.zeros_like(o_ref)

    up(j)
    activate_down()

    @pl.when(j < X1_SLABS)
    def _():
        rows = pl.ds(pl.multiple_of(j * (TM_FFN // X1_SLABS), TM_FFN // X1_SLABS), TM_FFN // X1_SLABS)
        o_ref[rows, :] += x1_ref[...]


def _ffn(h2, x1, wu, cw, cb, wd, tiles_per_seq):
    m = h2.shape[0]
    slab = TM_FFN // X1_SLABS
    cw3 = cw.reshape(FFN_CONV_K, 2 * N_FC, FC_FFN).transpose(1, 0, 2)
    cb3 = cb.reshape(2 * N_FC, 1, FC_FFN)
    return pl.pallas_call(
        functools.partial(_ffn_kernel, tiles_per_seq=tiles_per_seq),
        grid=(m // TM_FFN, N_FC),
        in_specs=[pl.BlockSpec((TM_FFN, D_MODEL), lambda i, j: (i, 0)),
                  pl.BlockSpec((slab, D_MODEL), lambda i, j: (i * X1_SLABS + jnp.minimum(j, X1_SLABS - 1), 0)),
                  pl.BlockSpec((D_MODEL, FC_FFN), lambda i, j: (0, j)),
                  pl.BlockSpec((D_MODEL, FC_FFN), lambda i, j: (0, N_FC + j)),
                  _resident((2 * N_FC, FFN_CONV_K, FC_FFN)), _resident((2 * N_FC, 1, FC_FFN)),
                  pl.BlockSpec((FC_FFN, D_MODEL), lambda i, j: (j, 0))],
        out_specs=pl.BlockSpec((TM_FFN, D_MODEL), lambda i, j: (i, 0)),
        out_shape=jax.ShapeDtypeStruct((m, D_MODEL), F32),
        scratch_shapes=[pltpu.VMEM((FC_FFN // LANES, TM_FFN + SUBLANES, LANES), F32),
                        pltpu.VMEM((FC_FFN // LANES, TM_FFN + SUBLANES, LANES), F32),
                        pltpu.VMEM((2 * (FC_FFN // LANES) * N_FC, SUBLANES, LANES), F32)],
        compiler_params=pltpu.CompilerParams(dimension_semantics=("arbitrary", "arbitrary"),
                                             vmem_limit_bytes=VMEM_LIMIT),
        name="conv_ffn",
    )(h2, x1, wu, wu, cw3, cb3, wd)


def _layer(x, attn_norm_w, w_in, q_norm_w, k_norm_w, sinks, conv_dw_w, conv_dw_b, conv_ln_w, conv_ln_b,
           w_out, ffn_norm_w, w_up, ffn_dw_w, ffn_dw_b, w_down):
    b, s, d = x.shape
    m = b * s
    head =lax.broadcasted_iota(jnp.int32, (MXU_DIM, MXU_DIM), 0) // HEAD_DIM
    gmat = jnp.where(head == head.T, 1.0 / HEAD_DIM, 0.0).astype(BF16)
    qw = jnp.tile(q_norm_w, MXU_DIM // HEAD_DIM)[None, :]
    kw = jnp.tile(k_norm_w, KV_DIM // HEAD_DIM)[None, :]
    q, kv, conv = _inproj(x, attn_norm_w[None, :], w_in.astype(BF16), gmat, qw, kw, conv_dw_w,
                          conv_dw_b[None, :], conv_ln_w[None, :], conv_ln_b[None, :])
    x1, h2 = _attention_outproj(sinks, q, kv, conv, x, w_out.astype(BF16), ffn_norm_w[None, :])
    out = _ffn(h2.reshape(m, d), x1.reshape(m, d), w_up, ffn_dw_w, ffn_dw_b, w_down, s // TM_FFN)
    return out.reshape(b, s, d)


def kernel(x, attn_norm_w, w_in, q_norm_w, k_norm_w, sinks, conv_dw_w, conv_dw_b, conv_ln_w, conv_ln_b, w_out,
           ffn_norm_w, w_up, ffn_dw_w, ffn_dw_b, w_down):
    for l in range(attn_norm_w.shape[0]):
        x = _layer(x, attn_norm_w[l], w_in[l], q_norm_w[l], k_norm_w[l], sinks[l], conv_dw_w[l], conv_dw_b[l],
                   conv_ln_w[l], conv_ln_b[l], w_out[l], ffn_norm_w[l], w_up[l], ffn_dw_w[l], ffn_dw_b[l],
                   w_down[l])
    return x
```
